```python
import jax, jax.numpy as jnp
from jax import lax
import numpy as np

D_MODEL = 1024
BATCH = 8
SEQ = 2048
DEPTH = 1
DEC_BATCH = 128
DEC_SEQ = 1
PAST_LEN = 16384
PAGE_SIZE = 128

N_META = 16
RET_HEADS = 4
RET_DK = 128
RET_DV = 256
RET_CHUNK = 128
MLA_HEADS = 8
Q_LORA = 384
KV_LORA = 256
NOPE_DIM = 64
ROPE_DIM = 32
QK_DIM = NOPE_DIM + ROPE_DIM
V_DIM = 64
ATTN_BLOCK = 128
D_FF = 4 * D_MODEL
ROPE_BASE = 10000.0
EPS = 1e-6

RET_QK_W = RET_HEADS * RET_DK
RET_V_W = RET_HEADS * RET_DV
MLA_V_W = MLA_HEADS * V_DIM
SPLITS = (RET_QK_W, RET_QK_W, RET_V_W, RET_V_W, Q_LORA, KV_LORA, ROPE_DIM, D_MODEL, D_MODEL)
IN_W = sum(SPLITS)

kernel_name = "retnet_mla_gated_hybrid_step"


def rms_norm(x, g=None):
    xf = x.astype(jnp.float32)
    y = xf * lax.rsqrt(jnp.mean(xf * xf, axis=-1, keepdims=True) + EPS)
    if g is not None:
        y = y * g.astype(jnp.float32)
    return y.astype(x.dtype)


def rotary(x, pos):
    half = x.shape[-1] // 2
    inv = ROPE_BASE ** (-jnp.arange(half, dtype=jnp.float32) / half)
    ang = pos.astype(jnp.float32)[:, None] * inv[None, :]
    ang = ang.reshape((ang.shape[0],) + (1,) * (x.ndim - 3) + (half,))
    cos, sin = jnp.cos(ang), jnp.sin(ang)
    x1 = x[..., :half].astype(jnp.float32)
    x2 = x[..., half:].astype(jnp.float32)
    return jnp.concatenate([x1 * cos - x2 * sin, x1 * sin + x2 * cos], axis=-1).astype(x.dtype)


def split_cols(z):
    idx = [int(i) for i in np.cumsum(SPLITS)[:-1]]
    return jnp.split(z, idx, axis=-1)


def retention_log_gamma():
    return jnp.log1p(-jnp.exp2(-5.0 - jnp.arange(RET_HEADS, dtype=jnp.float32)))


def retention_chunk(q, k, v, s_prev, log_gamma):
    C = q.shape[1]
    qf, kf, vf = q.astype(jnp.float32), k.astype(jnp.float32), v.astype(jnp.float32)
    idx = jnp.arange(C, dtype=jnp.float32)
    diff = idx[:, None] - idx[None, :]
    decay = jnp.where(diff[None] >= 0, jnp.exp(jnp.maximum(diff, 0.0)[None] * log_gamma[:, None, None]), 0.0)
    scores = jnp.einsum('bihd,bjhd->bhij', qf, kf) * decay[None]
    o_intra = jnp.einsum('bhij,bjhe->bihe', scores, vf)
    q_decay = jnp.exp((idx + 1.0)[:, None] * log_gamma[None, :])
    o_inter = jnp.einsum('bihd,bhde->bihe', qf, s_prev) * q_decay[None, :, :, None]
    k_decay = jnp.exp((C - 1.0 - idx)[None, :] * log_gamma[:, None])
    s_new = jnp.exp(C * log_gamma)[None, :, None, None] * s_prev + jnp.einsum('bjhd,hj,bjhe->bhde', kf, k_decay, vf)
    return o_intra + o_inter, s_new


def mla_keys(c, kr, w_uk, g_kn):
    k_nope = jnp.einsum('bsr,rhd->bshd', c, w_uk)
    k_rope = jnp.broadcast_to(kr[:, :, None, :], k_nope.shape[:3] + (ROPE_DIM,)).astype(k_nope.dtype)
    return rms_norm(jnp.concatenate([k_nope, k_rope], axis=-1), g_kn)


def mixer_inputs(h, pos, lp):
    B, T, _ = h.shape
    xn = rms_norm(h, lp['norm_mix'])
    rq, rk, rv, rg, qa, ckv, kr, gr, gm = split_cols(xn @ lp['w_in'])
    rq = rotary(rq.reshape(B, T, RET_HEADS, RET_DK), pos)
    rk = rotary(rk.reshape(B, T, RET_HEADS, RET_DK), pos) * (RET_DK ** -0.5)
    rv = rv.reshape(B, T, RET_HEADS, RET_DV)
    q = (rms_norm(qa, lp['g_qa']) @ lp['w_qb']).reshape(B, T, MLA_HEADS, QK_DIM)
    q = jnp.concatenate([q[..., :NOPE_DIM], rotary(q[..., NOPE_DIM:], pos)], axis=-1)
    q = rms_norm(q, lp['g_qn'])
    c = rms_norm(ckv, lp['g_kva'])
    kr = rotary(kr, pos)
    return rq, rk, rv, rg, q, c, kr, gr, gm


def finish_layer(h, o_ret, rg, o_mla, gr, gm, lp):
    B, T, _ = h.shape
    o_ret = rms_norm(o_ret).reshape(B, T, RET_V_W)
    r = (jax.nn.silu(rg.astype(jnp.float32)) * o_ret).astype(h.dtype) @ lp['w_ret_o']
    m = o_mla.reshape(B, T, MLA_V_W).astype(h.dtype) @ lp['w_mla_o']
    mixed = jax.nn.sigmoid(gr) * r + jax.nn.sigmoid(gm) * m
    h = h + (mixed @ lp['w_out']).astype(h.dtype)
    u = jax.nn.relu(rms_norm(h, lp['norm_ffn']) @ lp['w_up'])
    return h + ((u * u) @ lp['w_down']).astype(h.dtype)


def mla_prompt(q, c, kr, lp):
    B, L = q.shape[:2]
    scale = QK_DIM ** -0.5
    k = mla_keys(c, kr, lp['w_uk'], lp['g_kn']).astype(jnp.float32)
    v = jnp.einsum('blr,rhd->blhd', c.astype(jnp.float32), lp['w_uv'].astype(jnp.float32))
    qf = q.astype(jnp.float32)

    def attend(qb, q_pos, kk, vv, k_pos):
        s = jnp.einsum('bqhd,bkhd->bhqk', qb, kk) * scale
        s = jnp.where(k_pos[None, :] <= q_pos[:, None], s, -jnp.inf)
        p = jax.nn.softmax(s, axis=-1)
        return jnp.einsum('bhqk,bkhd->bqhd', p, vv)

    meta_pos = jnp.arange(N_META)
    o_meta = attend(qf[:, :N_META], meta_pos, k[:, :N_META], v[:, :N_META], meta_pos)
    n_blocks = (L - N_META) // ATTN_BLOCK
    k_pos = jnp.arange(L)

    def block(i):
        start = N_META + i * ATTN_BLOCK
        qb = lax.dynamic_slice_in_dim(qf, start, ATTN_BLOCK, axis=1)
        return attend(qb, start + jnp.arange(ATTN_BLOCK), k, v, k_pos)

    o_blocks = lax.map(block, jnp.arange(n_blocks))
    o_real = o_blocks.transpose(1, 0, 2, 3, 4).reshape(B, L - N_META, MLA_HEADS, V_DIM)
    return jnp.concatenate([o_meta, o_real], axis=1)


def mla_decode(q, c_new, kr_new, pool_c, pool_kr, page_table, lp):
    T = q.shape[1]
    scale = QK_DIM ** -0.5
    qf = q.astype(jnp.float32)

    def block_scores(c, kr):
        k = mla_keys(c, kr, lp['w_uk'], lp['g_kn']).astype(jnp.float32)
        return jnp.einsum('bqhd,bkhd->bhqk', qf, k) * scale

    s = block_scores(c_new, kr_new)
    s = jnp.where(jnp.tril(jnp.ones((T, T), dtype=bool)), s, -jnp.inf)
    m = s.max(axis=-1)
    p = jnp.exp(s - m[..., None])
    l = p.sum(axis=-1)
    acc = jnp.einsum('bhqk,bkr->bhqr', p, c_new.astype(jnp.float32))

    def step(carry, pidx):
        m, l, acc = carry
        c = pool_c[pidx]
        kr = pool_kr[pidx]
        s = block_scores(c, kr)
        m_new = jnp.maximum(m, s.max(axis=-1))
        alpha = jnp.exp(m - m_new)
        p = jnp.exp(s - m_new[..., None])
        acc = alpha[..., None] * acc + jnp.einsum('bhqk,bkr->bhqr', p, c.astype(jnp.float32))
        return (m_new, alpha * l + p.sum(axis=-1), acc), None

    (m, l, acc), _ = lax.scan(step, (m, l, acc), page_table.T)
    lat = acc / l[..., None]
    return jnp.einsum('bhqr,rhd->bqhd', lat, lp['w_uv'].astype(jnp.float32))


def setup_inputs(seed: int = 0) -> dict:
    key = jax.random.key(seed)
    ks = jax.random.split(key, 22)
    n_pages = PAST_LEN // PAGE_SIZE
    used = DEC_BATCH * n_pages
    n_pool = used + max(1, used // 4)
    f32 = jnp.float32
    nrm = lambda k, shape, s: jax.random.normal(k, shape, f32) * s
    gain = lambda k, shape: 1.0 + 0.02 * jax.random.normal(k, shape, f32)
    page_table = jax.random.permutation(ks[5], n_pool)[:used].reshape(DEC_BATCH, n_pages).astype(jnp.int32)
    return {
        'x_prompt': nrm(ks[0], (BATCH, SEQ, D_MODEL), 1.0),
        'x_sample': nrm(ks[1], (DEC_BATCH, DEC_SEQ, D_MODEL), 1.0),
        'cache_ckv': nrm(ks[2], (DEPTH, n_pool, PAGE_SIZE, KV_LORA), 1.0),
        'cache_krope': nrm(ks[3], (DEPTH, n_pool, PAGE_SIZE, ROPE_DIM), 1.0),
        'state_ret': nrm(ks[4], (DEPTH, DEC_BATCH, RET_HEADS, RET_DK, RET_DV), 0.3),
        'page_table': page_table,
        'meta_tokens': nrm(ks[6], (N_META, D_MODEL), 1.0),
        'norm_mix': gain(ks[7], (DEPTH, D_MODEL)),
        'w_in': nrm(ks[8], (DEPTH, D_MODEL, IN_W), D_MODEL ** -0.5),
        'g_qa': gain(ks[9], (DEPTH, Q_LORA)),
        'w_qb': nrm(ks[10], (DEPTH, Q_LORA, MLA_HEADS * QK_DIM), Q_LORA ** -0.5),
        'g_kva': gain(ks[11], (DEPTH, KV_LORA)),
        'w_uk': nrm(ks[12], (DEPTH, KV_LORA, MLA_HEADS, NOPE_DIM), KV_LORA ** -0.5),
        'w_uv': nrm(ks[13], (DEPTH, KV_LORA, MLA_HEADS, V_DIM), KV_LORA ** -0.5),
        'g_qn': gain(ks[14], (DEPTH, QK_DIM)),
        'g_kn': gain(ks[15], (DEPTH, QK_DIM)),
        'w_ret_o': nrm(ks[16], (DEPTH, RET_V_W, D_MODEL), RET_V_W ** -0.5),
        'w_mla_o': nrm(ks[17], (DEPTH, MLA_V_W, D_MODEL), MLA_V_W ** -0.5),
        'w_out': nrm(ks[18], (DEPTH, D_MODEL, D_MODEL), D_MODEL ** -0.5),
        'norm_ffn': gain(ks[19], (DEPTH, D_MODEL)),
        'w_up': nrm(ks[20], (DEPTH, D_MODEL, D_FF), D_MODEL ** -0.5),
        'w_down': nrm(ks[21], (DEPTH, D_FF, D_MODEL), D_FF ** -0.5),
    }


def reference(x_prompt, x_sample, cache_ckv, cache_krope, state_ret, page_table, meta_tokens,
              norm_mix, w_in, g_qa, w_qb, g_kva, w_uk, w_uv, g_qn, g_kn,
              w_ret_o, w_mla_o, w_out, norm_ffn, w_up, w_down):
    B, S, _ = x_prompt.shape
    T = x_sample.shape[1]
    past = page_table.shape[1] * PAGE_SIZE
    log_gamma = retention_log_gamma()
    pos_p = jnp.arange(N_META + S)
    pos_s = past + jnp.arange(T)
    n_chunks = S // RET_CHUNK

    hp = jnp.concatenate([jnp.broadcast_to(meta_tokens[None], (B, N_META, D_MODEL)).astype(x_prompt.dtype), x_prompt], axis=1)
    hs = x_sample
    ckv_p, kr_p, st_p, ckv_s, kr_s, st_s = [], [], [], [], [], []

    for layer in range(DEPTH):
        lp = {'norm_mix': norm_mix[layer], 'w_in': w_in[layer], 'g_qa': g_qa[layer], 'w_qb': w_qb[layer],
              'g_kva': g_kva[layer], 'w_uk': w_uk[layer], 'w_uv': w_uv[layer], 'g_qn': g_qn[layer],
              'g_kn': g_kn[layer], 'w_ret_o': w_ret_o[layer], 'w_mla_o': w_mla_o[layer], 'w_out': w_out[layer],
              'norm_ffn': norm_ffn[layer], 'w_up': w_up[layer], 'w_down': w_down[layer]}

        rq, rk, rv, rg, q, c, kr, gr, gm = mixer_inputs(hp, pos_p, lp)
        zero = jnp.zeros((B, RET_HEADS, RET_DK, RET_DV), jnp.float32)
        o_meta, s0 = retention_chunk(rq[:, :N_META], rk[:, :N_META], rv[:, :N_META], zero, log_gamma)

        def to_chunks(t):
            return t[:, N_META:].reshape((B, n_chunks, RET_CHUNK) + t.shape[2:]).transpose(1, 0, 2, 3, 4)

        def ret_step(s, qkv):
            o, s = retention_chunk(qkv[0], qkv[1], qkv[2], s, log_gamma)
            return s, o

        s_fin, o_chunks = lax.scan(ret_step, s0, (to_chunks(rq), to_chunks(rk), to_chunks(rv)))
        o_ret = jnp.concatenate([o_meta, o_chunks.transpose(1, 0, 2, 3, 4).reshape(B, S, RET_HEADS, RET_DV)], axis=1)
        o_mla = mla_prompt(q, c, kr, lp)
        hp = finish_layer(hp, o_ret, rg, o_mla, gr, gm, lp)
        ckv_p.append(c)
        kr_p.append(kr)
        st_p.append(s_fin)

        rq, rk, rv, rg, q, c, kr, gr, gm = mixer_inputs(hs, pos_s, lp)
        o_ret, s_new = retention_chunk(rq, rk, rv, state_ret[layer].astype(jnp.float32), log_gamma)
        o_mla = mla_decode(q, c, kr, cache_ckv[layer], cache_krope[layer], page_table, lp)
        hs = finish_layer(hs, o_ret, rg, o_mla, gr, gm, lp)
        ckv_s.append(c)
        kr_s.append(kr)
        st_s.append(s_new)

    y_prompt = hp[:, N_META:]
    y_sample = hs
    return (y_prompt, y_sample, jnp.stack(ckv_p), jnp.stack(kr_p), jnp.stack(st_p),
            jnp.stack(ckv_s), jnp.stack(kr_s), jnp.stack(st_s))
```

```python
import functools

import numpy as np
import jax
import jax.numpy as jnp
from jax import lax
from jax.experimental import pallas as pl
from jax.experimental.pallas import tpu as pltpu

D_MODEL = 1024
N_META = 16
RET_HEADS = 4
RET_DK = 128
RET_DV = 256
RET_CHUNK = 128
MLA_HEADS = 8
Q_LORA = 384
KV_LORA = 256
NOPE_DIM = 64
ROPE_DIM = 32
QK_DIM = NOPE_DIM + ROPE_DIM
V_DIM = 64
D_FF = 4 * D_MODEL
PAGE_SIZE = 128
ROPE_BASE = 10000.0
EPS = 1e-6

RET_QK_W = RET_HEADS * RET_DK
RET_V_W = RET_HEADS * RET_DV
MLA_V_W = MLA_HEADS * V_DIM
SPLITS = (RET_QK_W, RET_QK_W, RET_V_W, RET_V_W, Q_LORA, KV_LORA, ROPE_DIM, D_MODEL, D_MODEL)

LANES = 128
HEAD_PAD = LANES
MLA_PAD_W = MLA_HEADS * HEAD_PAD
_SEG_W = (RET_QK_W, RET_QK_W, RET_V_W, RET_V_W, Q_LORA, KV_LORA, LANES, D_MODEL, D_MODEL)
_SEG_O = tuple(int(v) for v in np.cumsum((0,) + _SEG_W))
IN_PAD_W = _SEG_O[-1]
VMEM_LIMIT = 56 * 1024 * 1024

F32 = jnp.float32
BF16 = jnp.bfloat16
_NT = (((1,), (1,)), ((), ()))
_TN = (((0,), (0,)), ((), ()))


def _dot(a, b):
    return jnp.dot(a, b, preferred_element_type=F32)


def _dot_nt(a, b):
    return lax.dot_general(a, b, _NT, preferred_element_type=F32)


def _const_spec(shape):
    zeros = (0,) * len(shape)
    return pl.BlockSpec(shape, lambda *_: zeros, pipeline_mode=pl.Buffered(1))


def _params(sem):
    return pltpu.CompilerParams(dimension_semantics=sem, vmem_limit_bytes=VMEM_LIMIT)


def _proj_kernel(x_ref, cosr_ref, sinr_ref, cm_ref, sa_ref, sb_ref, nmix_ref, win_ref, gqa_ref,
                 wqb_ref, gkva_ref, wukp_ref, wuv_ref, gq_ref, gk_ref,
                 rq_ref, rk_ref, rv_ref, rg_ref, gr_ref, gm_ref, q_ref, k_ref, v_ref, c_ref, kr_ref):
    x = x_ref[...]
    xn = x * lax.rsqrt(jnp.mean(x * x, axis=-1, keepdims=True) + EPS) * nmix_ref[...]
    xb = xn.astype(BF16)

    def seg(i):
        return _dot(xb, win_ref[:, _SEG_O[i]:_SEG_O[i + 1]])

    cosr, sinr = cosr_ref[...], sinr_ref[...]
    cm, sa, sb = cm_ref[...], sa_ref[...], sb_ref[...]

    def ret_rotary(z, out_ref, scale):
        for h in range(RET_HEADS):
            zh = z[:, h * RET_DK:(h + 1) * RET_DK]
            r = zh * cosr + pltpu.roll(zh, RET_DK // 2, 1) * sinr
            if scale is not None:
                r = r * scale
            out_ref[:, h * RET_DK:(h + 1) * RET_DK] = r.astype(out_ref.dtype)

    def mla_rotary(zh):
        half = ROPE_DIM // 2
        return zh * cm + pltpu.roll(zh, HEAD_PAD - half, 1) * sa + pltpu.roll(zh, half, 1) * sb

    def head_norm(zh, g):
        ss = jnp.sum(zh * zh, axis=-1, keepdims=True)
        return zh * lax.rsqrt(ss * (1.0 / QK_DIM) + EPS) * g

    ret_rotary(seg(0), rq_ref, None)
    ret_rotary(seg(1), rk_ref, RET_DK ** -0.5)
    rv_ref[...] = seg(2).astype(rv_ref.dtype)
    rg_ref[...] = seg(3).astype(rg_ref.dtype)
    gr_ref[...] = seg(7).astype(gr_ref.dtype)
    gm_ref[...] = seg(8).astype(gm_ref.dtype)

    qa = seg(4)
    qn = qa * lax.rsqrt(jnp.mean(qa * qa, axis=-1, keepdims=True) + EPS) * gqa_ref[...]
    qp = _dot(qn.astype(BF16), wqb_ref[...])
    gq = gq_ref[...]
    for h in range(MLA_HEADS):
        qh = mla_rotary(qp[:, h * HEAD_PAD:(h + 1) * HEAD_PAD])
        q_ref[:, h * HEAD_PAD:(h + 1) * HEAD_PAD] = head_norm(qh, gq).astype(q_ref.dtype)

    ckv = seg(5)
    c = ckv * lax.rsqrt(jnp.mean(ckv * ckv, axis=-1, keepdims=True) + EPS) * gkva_ref[...]
    c_ref[...] = c
    cb = c.astype(BF16)
    krp = mla_rotary(seg(6))
    kr_ref[...] = krp[:, NOPE_DIM:QK_DIM]
    kp = _dot(cb, wukp_ref[...])
    gk = gk_ref[...]
    for h in range(MLA_HEADS):
        kh = kp[:, h * HEAD_PAD:(h + 1) * HEAD_PAD] + krp
        k_ref[:, h * HEAD_PAD:(h + 1) * HEAD_PAD] = head_norm(kh, gk).astype(k_ref.dtype)
    v_ref[...] = _dot(cb, wuv_ref[...]).astype(v_ref.dtype)


def _proj(x, tabs, w, tm, mid_dtype):
    rows = x.shape[0]
    period = tabs[0].shape[0] // tm
    row = lambda i: (i, 0)
    tab_spec = pl.BlockSpec((tm, LANES), lambda i: (i % period, 0))
    widths = (RET_QK_W, RET_QK_W, RET_V_W, RET_V_W, D_MODEL, D_MODEL, MLA_PAD_W, MLA_PAD_W, MLA_V_W)
    out_shape = [jax.ShapeDtypeStruct((rows, n), mid_dtype) for n in widths]
    out_shape += [jax.ShapeDtypeStruct((rows, KV_LORA), F32), jax.ShapeDtypeStruct((rows, ROPE_DIM), F32)]
    out_specs = [pl.BlockSpec((tm, n), row) for n in widths]
    out_specs += [pl.BlockSpec((tm, KV_LORA), row), pl.BlockSpec((tm, ROPE_DIM), row)]
    consts = (w['norm_mix'], w['w_in'], w['g_qa'], w['w_qb'], w['g_kva'], w['w_ukp'], w['w_uv'],
              w['gq'], w['gk'])
    return pl.pallas_call(
        _proj_kernel,
        grid=(rows // tm,),
        in_specs=[pl.BlockSpec((tm, D_MODEL), row)] + [tab_spec] * 5 + [_const_spec(a.shape) for a in consts],
        out_specs=out_specs,
        out_shape=out_shape,
        compiler_params=_params(("parallel",)),
        name="proj",
    )(x, *tabs, *consts)


def _retention_kernel(q_ref, k_ref, v_ref, rg_ref, dec_ref, qdec_ref, kdec_ref, sdec_ref, s0_ref,
                      o_ref, st_ref, s_scr):
    ci = pl.program_id(1)

    @pl.when(ci == 0)
    def _():
        s_scr[...] = s0_ref[0]

    for h in range(RET_HEADS):
        qh = q_ref[:, h * RET_DK:(h + 1) * RET_DK]
        kh = k_ref[:, h * RET_DK:(h + 1) * RET_DK]
        vh = v_ref[:, h * RET_DV:(h + 1) * RET_DV]
        scores = _dot_nt(qh, kh) * dec_ref[h]
        o = _dot(scores.astype(BF16), vh)
        s_prev = s_scr[h]
        o = o + _dot(qh, s_prev.astype(BF16)) * qdec_ref[h]
        kd = (kh.astype(F32) * kdec_ref[h]).astype(BF16)
        s_scr[h] = sdec_ref[h] * s_prev + lax.dot_general(kd, vh, _TN, preferred_element_type=F32)
        on = o * lax.rsqrt(jnp.mean(o * o, axis=-1, keepdims=True) + EPS)
        g = rg_ref[:, h * RET_DV:(h + 1) * RET_DV].astype(F32)
        o_ref[:, h * RET_DV:(h + 1) * RET_DV] = (g * jax.nn.sigmoid(g) * on).astype(o_ref.dtype)

    @pl.when(ci == pl.num_programs(1) - 1)
    def _():
        st_ref[0] = s_scr[...]


def _retention(rq, rk, rv, rg, s0, dtabs):
    nb, s = rq.shape[:2]
    c = RET_CHUNK
    blk = lambda n: pl.BlockSpec((None, c, n), lambda b, i: (b, i, 0))
    st_shape = (1, RET_HEADS, RET_DK, RET_DV)
    return pl.pallas_call(
        _retention_kernel,
        grid=(nb, s // c),
        in_specs=[blk(RET_QK_W), blk(RET_QK_W), blk(RET_V_W), blk(RET_V_W)]
        + [_const_spec(t.shape) for t in dtabs] + [_const_spec(st_shape)],
        out_specs=[blk(RET_V_W), pl.BlockSpec(st_shape, lambda b, i: (b, 0, 0, 0))],
        out_shape=[jax.ShapeDtypeStruct((nb, s, RET_V_W), BF16),
                   jax.ShapeDtypeStruct((nb,) + st_shape[1:], F32)],
        scratch_shapes=[pltpu.VMEM(st_shape[1:], F32)],
        compiler_params=_params(("parallel", "arbitrary")),
        name="retention",
    )(rq, rk, rv, rg, *dtabs, s0)


ATT_T = 256


def _attention_kernel(q_ref, k_ref, v_ref, km_ref, vm_ref, o_ref):
    qi = pl.program_id(1)
    pad = km_ref.shape[0] - N_META
    row = lax.broadcasted_iota(jnp.int32, (ATT_T, ATT_T), 0)
    col = lax.broadcasted_iota(jnp.int32, (ATT_T, ATT_T), 1)
    meta_ok = lax.broadcasted_iota(jnp.int32, (ATT_T, km_ref.shape[0]), 1) >= pad

    for h in range(MLA_HEADS):
        qh = q_ref[:, h * HEAD_PAD:(h + 1) * HEAD_PAD]
        s = jnp.where(meta_ok, _dot_nt(qh, km_ref[:, h * HEAD_PAD:(h + 1) * HEAD_PAD]), -jnp.inf)
        m = jnp.max(s, axis=-1, keepdims=True)
        p = jnp.exp(s - m)
        l = jnp.sum(p, axis=-1, keepdims=True)
        acc = _dot(p.astype(BF16), vm_ref[:, h * V_DIM:(h + 1) * V_DIM])

        def update(j, carry, masked):
            m, l, acc = carry
            start = pl.multiple_of(j * ATT_T, ATT_T)
            s = _dot_nt(qh, k_ref[pl.ds(start, ATT_T), h * HEAD_PAD:(h + 1) * HEAD_PAD])
            if masked:
                s = jnp.where(col <= row, s, -jnp.inf)
            m_new = jnp.maximum(m, jnp.max(s, axis=-1, keepdims=True))
            alpha = jnp.exp(m - m_new)
            p = jnp.exp(s - m_new)
            l = alpha * l + jnp.sum(p, axis=-1, keepdims=True)
            acc = alpha * acc + _dot(p.astype(BF16), v_ref[pl.ds(start, ATT_T), h * V_DIM:(h + 1) * V_DIM])
            return m_new, l, acc

        carry = lax.fori_loop(0, qi, functools.partial(update, masked=False), (m, l, acc))
        m, l, acc = update(qi, carry, True)
        o_ref[:, h * V_DIM:(h + 1) * V_DIM] = (acc / l).astype(o_ref.dtype)


def _attention(q, k, v, km, vm):
    nb, s = q.shape[:2]
    return pl.pallas_call(
        _attention_kernel,
        grid=(nb, s // ATT_T),
        in_specs=[pl.BlockSpec((None, ATT_T, MLA_PAD_W), lambda b, i: (b, i, 0)),
                  pl.BlockSpec((None, s, MLA_PAD_W), lambda b, i: (b, 0, 0)),
                  pl.BlockSpec((None, s, MLA_V_W), lambda b, i: (b, 0, 0)),
                  _const_spec(km.shape), _const_spec(vm.shape)],
        out_specs=pl.BlockSpec((None, ATT_T, MLA_V_W), lambda b, i: (b, i, 0)),
        out_shape=jax.ShapeDtypeStruct((nb, s, MLA_V_W), BF16),
        compiler_params=_params(("parallel", "arbitrary")),
        name="attention",
    )(q, k, v, km, vm)


FF_BLOCK = 1024


def _finish_kernel(x_ref, ret_ref, mla_ref, gr_ref, gm_ref, wro_ref, wmo_ref, wout_ref, nffn_ref,
                   wup_ref, wdn_ref, y_ref):
    r = _dot(ret_ref[...].astype(BF16), wro_ref[...])
    m = _dot(mla_ref[...].astype(BF16), wmo_ref[...])
    mixed = jax.nn.sigmoid(gr_ref[...].astype(F32)) * r + jax.nn.sigmoid(gm_ref[...].astype(F32)) * m
    h = x_ref[...] + _dot(mixed.astype(BF16), wout_ref[...])
    hn = h * lax.rsqrt(jnp.mean(h * h, axis=-1, keepdims=True) + EPS) * nffn_ref[...]
    hb = hn.astype(BF16)
    y = h
    for f in range(D_FF // FF_BLOCK):
        u = jnp.maximum(_dot(hb, wup_ref[:, f * FF_BLOCK:(f + 1) * FF_BLOCK]), 0.0)
        y = y + _dot((u * u).astype(BF16), wdn_ref[f * FF_BLOCK:(f + 1) * FF_BLOCK, :])
    y_ref[...] = y


def _finish(x, ret_mid, o_mla, gr, gm, w, tm):
    rows = x.shape[0]
    row = lambda i: (i, 0)
    consts = (w['w_ret_o'], w['w_mla_o'], w['w_out'], w['norm_ffn'], w['w_up'], w['w_down'])
    return pl.pallas_call(
        _finish_kernel,
        grid=(rows // tm,),
        in_specs=[pl.BlockSpec((tm, D_MODEL), row), pl.BlockSpec((tm, RET_V_W), row),
                  pl.BlockSpec((tm, MLA_V_W), row), pl.BlockSpec((tm, D_MODEL), row),
                  pl.BlockSpec((tm, D_MODEL), row)] + [_const_spec(a.shape) for a in consts],
        out_specs=pl.BlockSpec((tm, D_MODEL), row),
        out_shape=jax.ShapeDtypeStruct((rows, D_MODEL), F32),
        compiler_params=_params(("parallel",)),
        name="finish",
    )(x, ret_mid, o_mla, gr, gm, *consts)


RET_DEC_GROUP = 8


def _column(row_vec, eye):
    return jnp.sum(jnp.where(eye, row_vec, 0.0), axis=-1, keepdims=True)


def _ret_decode_kernel(q_ref, k_ref, v_ref, rg_ref, gam_ref, s_ref, o_ref, sn_ref):
    eye = (lax.broadcasted_iota(jnp.int32, (RET_DK, RET_DK), 0)
           == lax.broadcasted_iota(jnp.int32, (RET_DK, RET_DK), 1))
    for b in range(RET_DEC_GROUP):
        for h in range(RET_HEADS):
            k_col = _column(k_ref[b:b + 1, h * RET_DK:(h + 1) * RET_DK].astype(F32), eye)
            q_col = _column(q_ref[b:b + 1, h * RET_DK:(h + 1) * RET_DK].astype(F32), eye)
            v_row = v_ref[b:b + 1, h * RET_DV:(h + 1) * RET_DV].astype(F32)
            gam = gam_ref[h]
            s_prev = s_ref[b, h]
            kv = k_col * v_row
            sn_ref[b, h] = gam * s_prev + kv
            o = jnp.sum(q_col * kv, axis=0, keepdims=True) + gam * jnp.sum(q_col * s_prev, axis=0, keepdims=True)
            on = o * lax.rsqrt(jnp.mean(o * o, axis=-1, keepdims=True) + EPS)
            g = rg_ref[b:b + 1, h * RET_DV:(h + 1) * RET_DV].astype(F32)
            o_ref[b:b + 1, h * RET_DV:(h + 1) * RET_DV] = (g * jax.nn.sigmoid(g) * on).astype(o_ref.dtype)


def _ret_decode(rq, rk, rv, rg, gam, state):
    nb = rq.shape[0]
    g = RET_DEC_GROUP
    row = lambda i: (i, 0)
    st_spec = pl.BlockSpec((g, RET_HEADS, RET_DK, RET_DV), lambda i: (i, 0, 0, 0))
    return pl.pallas_call(
        _ret_decode_kernel,
        grid=(nb // g,),
        in_specs=[pl.BlockSpec((g, RET_QK_W), row), pl.BlockSpec((g, RET_QK_W), row),
                  pl.BlockSpec((g, RET_V_W), row), pl.BlockSpec((g, RET_V_W), row),
                  _const_spec(gam.shape), st_spec],
        out_specs=[pl.BlockSpec((g, RET_V_W), row), st_spec],
        out_shape=[jax.ShapeDtypeStruct((nb, RET_V_W), F32), jax.ShapeDtypeStruct(state.shape, F32)],
        compiler_params=_params(("parallel",)),
        name="ret_decode",
    )(rq, rk, rv, rg, gam, state)


def _absorb_kernel(q_ref, k_ref, gk_ref, wukp_ref, qt_ref, qg_ref, s0_ref):
    q = q_ref[...]
    lane = lax.broadcasted_iota(jnp.int32, (q.shape[0], LANES), 1)
    s0 = jnp.zeros((q.shape[0], LANES), F32)
    gk = gk_ref[...]
    for h in range(MLA_HEADS):
        sl = slice(h * HEAD_PAD, (h + 1) * HEAD_PAD)
        qh = q[:, sl]
        s0 = s0 + jnp.where(lane == h, jnp.sum(qh * k_ref[:, sl], axis=-1, keepdims=True), 0.0)
        qg = qh * gk
        qg_ref[:, sl] = qg
        qt_ref[h] = lax.dot_general(qg, wukp_ref[:, sl], _NT, precision=lax.Precision.HIGHEST,
                                    preferred_element_type=F32)
    s0_ref[...] = s0


def _absorb(q, k, gk, wukp):
    nb = q.shape[0]
    return pl.pallas_call(
        _absorb_kernel,
        out_shape=[jax.ShapeDtypeStruct((MLA_HEADS, nb, KV_LORA), F32),
                   jax.ShapeDtypeStruct((nb, MLA_PAD_W), F32),
                   jax.ShapeDtypeStruct((nb, LANES), F32)],
        compiler_params=pltpu.CompilerParams(vmem_limit_bytes=VMEM_LIMIT),
        name="absorb",
    )(q, k, gk, wukp)


DEC_PAGES = 8
DEC_T = DEC_PAGES * PAGE_SIZE
DEC_SUB = 256
UK_ROWS = MLA_HEADS * NOPE_DIM
QT_ROWS = 16
ROPE_LHS_ROWS = 48


def _mla_decode_kernel(pt_ref, lhsw_ref, qt_ref, arope_ref, s0_ref, cnew_ref, wuv_ref, ckv_hbm, kr_hbm,
                       o_ref, lhs_scr, cbuf, kbuf, sem):
    b = pl.program_id(0)
    nb = pl.num_programs(0)
    n_groups = pt_ref.shape[1] // DEC_PAGES

    def page_copies(seq, g, slot):
        copies = []
        for i in range(DEC_PAGES):
            page = pt_ref[seq, g * DEC_PAGES + i]
            rows = pl.ds(i * PAGE_SIZE, PAGE_SIZE)
            copies.append(pltpu.make_async_copy(ckv_hbm.at[0, page], cbuf.at[slot, rows], sem.at[0, slot]))
            copies.append(pltpu.make_async_copy(kr_hbm.at[0, page], kbuf.at[slot, rows], sem.at[1, slot]))
        return copies

    def start_group(seq, g, slot):
        for cp in page_copies(seq, g, slot):
            cp.start()

    def wait_group(slot):
        for cp in page_copies(0, 0, slot):
            cp.wait()

    @pl.when(b == 0)
    def _():
        start_group(0, 0, 0)

    lhs_scr[0:UK_ROWS, :] = lhsw_ref[...]
    lhs_scr[UK_ROWS:UK_ROWS + QT_ROWS, :] = qt_ref[0]
    arope = arope_ref[0]

    def group_update(slot, carry):
        m, l, acc = carry
        cb = cbuf[slot].astype(BF16)
        kb = kbuf[slot].astype(BF16)
        pieces = []
        for t in range(DEC_T // DEC_SUB):
            tok = slice(t * DEC_SUB, (t + 1) * DEC_SUB)
            big = _dot_nt(lhs_scr[...], cb[tok])
            sq = [big[d * MLA_HEADS:(d + 1) * MLA_HEADS] for d in range(NOPE_DIM)]
            sq = [x * x for x in sq]
            while len(sq) > 1:
                sq = [sq[i] + sq[i + 1] for i in range(0, len(sq), 2)]
            rope = _dot_nt(arope, kb[tok])
            kr_t = rope[MLA_HEADS:MLA_HEADS + ROPE_DIM]
            ss = sq[0] + jnp.sum(kr_t * kr_t, axis=0, keepdims=True)
            num = big[UK_ROWS:UK_ROWS + MLA_HEADS] + rope[0:MLA_HEADS]
            pieces.append(num * lax.rsqrt(ss * (1.0 / QK_DIM) + EPS))
        s = jnp.concatenate(pieces, axis=-1)
        m_new = jnp.maximum(m, jnp.max(s, axis=-1, keepdims=True))
        alpha = jnp.exp(m - m_new)
        p = jnp.exp(s - m_new)
        l = alpha * l + jnp.sum(p, axis=-1, keepdims=True)
        acc = alpha * acc + _dot(p.astype(BF16), cb)
        return m_new, l, acc

    def pair(gg, carry):
        for slot in (0, 1):
            nxt = gg * 2 + slot + 1

            @pl.when(nxt < n_groups)
            def _():
                start_group(b, nxt, 1 - slot)

            if slot == 1:
                @pl.when(jnp.logical_and(nxt == n_groups, b + 1 < nb))
                def _():
                    start_group(b + 1, 0, 0)

            wait_group(slot)
            carry = group_update(slot, carry)
        return carry

    init = (s0_ref[0], jnp.ones((MLA_HEADS, 1), F32),
            jnp.broadcast_to(cnew_ref[0], (MLA_HEADS, KV_LORA)))
    m, l, acc = lax.fori_loop(0, n_groups // 2, pair, init)
    lat = acc / l
    full = _dot(lat.astype(BF16), wuv_ref[...])
    own = (lax.broadcasted_iota(jnp.int32, full.shape, 1) // V_DIM
           == lax.broadcasted_iota(jnp.int32, full.shape, 0))
    o_ref[0] = jnp.sum(jnp.where(own, full, 0.0), axis=0, keepdims=True)


def _mla_decode(page_table, lhsw, qt, arope, s0, c_new, wuv, cache_ckv, cache_krope):
    nb = page_table.shape[0]
    per_seq = lambda shape: pl.BlockSpec((1,) + shape, lambda i, pt: (i, 0, 0))
    const = lambda a: pl.BlockSpec(a.shape, lambda i, pt: (0,) * a.ndim, pipeline_mode=pl.Buffered(1))
    grid_spec = pltpu.PrefetchScalarGridSpec(
        num_scalar_prefetch=1,
        grid=(nb,),
        in_specs=[const(lhsw), per_seq((QT_ROWS, KV_LORA)), per_seq((ROPE_LHS_ROWS, ROPE_DIM)),
                  per_seq((MLA_HEADS, 1)), per_seq((1, KV_LORA)), const(wuv),
                  pl.BlockSpec(memory_space=pl.ANY), pl.BlockSpec(memory_space=pl.ANY)],
        out_specs=per_seq((1, MLA_V_W)),
        scratch_shapes=[pltpu.VMEM((UK_ROWS + QT_ROWS, KV_LORA), BF16),
                        pltpu.VMEM((2, DEC_T, KV_LORA), F32),
                        pltpu.VMEM((2, DEC_T, ROPE_DIM), F32),
                        pltpu.SemaphoreType.DMA((2, 2))],
    )
    return pl.pallas_call(
        _mla_decode_kernel,
        grid_spec=grid_spec,
        out_shape=jax.ShapeDtypeStruct((nb, 1, MLA_V_W), F32),
        compiler_params=_params(("arbitrary",)),
        name="mla_decode",
    )(page_table, lhsw, qt, arope, s0, c_new, wuv, cache_ckv, cache_krope)


def _rotary_tables(pos):
    pos = pos.astype(F32)[:, None]

    def cos_sin(half):
        inv = ROPE_BASE ** (-jnp.arange(half, dtype=F32) / half)
        ang = pos * inv[None, :]
        return jnp.cos(ang), jnp.sin(ang)

    cr, sr = cos_sin(RET_DK // 2)
    cosr = jnp.concatenate([cr, cr], axis=-1)
    sinr = jnp.concatenate([-sr, sr], axis=-1)
    c, s = cos_sin(ROPE_DIM // 2)
    n = pos.shape[0]
    half = ROPE_DIM // 2
    z = lambda w: jnp.zeros((n, w), F32)
    cm = jnp.concatenate([jnp.ones((n, NOPE_DIM), F32), c, c, z(HEAD_PAD - QK_DIM)], axis=-1)
    sa = jnp.concatenate([z(NOPE_DIM), -s, z(half), z(HEAD_PAD - QK_DIM)], axis=-1)
    sb = jnp.concatenate([z(NOPE_DIM), z(half), s, z(HEAD_PAD - QK_DIM)], axis=-1)
    return cosr, sinr, cm, sa, sb


def _decay_tables(c):
    lg = jnp.log1p(-jnp.exp2(-5.0 - jnp.arange(RET_HEADS, dtype=F32)))
    idx = jnp.arange(c, dtype=F32)
    diff = idx[:, None] - idx[None, :]
    dec = jnp.where(diff[None] >= 0, jnp.exp(jnp.maximum(diff, 0.0)[None] * lg[:, None, None]), 0.0)
    qdec = jnp.exp((idx + 1.0)[None, :] * lg[:, None])[:, :, None]
    kdec = jnp.exp((c - 1.0 - idx)[None, :] * lg[:, None])[:, :, None]
    sdec = jnp.exp(c * lg)[:, None, None]
    return dec, qdec, kdec, sdec


def _pad_heads(a):
    a = jnp.pad(a, [(0, 0)] * (a.ndim - 1) + [(0, HEAD_PAD - a.shape[-1])])
    return a.reshape(a.shape[:-2] + (a.shape[-2] * HEAD_PAD,))


def _layer_weights(norm_mix, w_in, g_qa, w_qb, g_kva, w_uk, w_uv, g_qn, g_kn, w_ret_o, w_mla_o, w_out,
                   norm_ffn, w_up, w_down):
    idx = [int(i) for i in np.cumsum(SPLITS)[:-1]]
    parts = jnp.split(w_in, idx, axis=-1)
    parts[6] = jnp.pad(parts[6], ((0, 0), (NOPE_DIM, HEAD_PAD - QK_DIM)))
    pad1 = lambda g: jnp.pad(g, (0, HEAD_PAD - QK_DIM))[None, :]
    wukp = _pad_heads(w_uk)
    return {
        'norm_mix': norm_mix[None, :],
        'w_in': jnp.concatenate(parts, axis=-1).astype(BF16),
        'g_qa': g_qa[None, :],
        'w_qb': _pad_heads(w_qb.reshape(Q_LORA, MLA_HEADS, QK_DIM)).astype(BF16),
        'g_kva': g_kva[None, :],
        'w_ukp': wukp.astype(BF16),
        'w_ukp_f32': wukp,
        'w_uk_t': w_uk.transpose(2, 1, 0).reshape(UK_ROWS, KV_LORA).astype(BF16),
        'w_uv': w_uv.reshape(KV_LORA, MLA_V_W).astype(BF16),
        'gq': pad1(g_qn) * (QK_DIM ** -0.5),
        'gk': pad1(g_kn),
        'w_ret_o': w_ret_o.astype(BF16),
        'w_mla_o': w_mla_o.astype(BF16),
        'w_out': w_out.astype(BF16),
        'norm_ffn': norm_ffn[None, :],
        'w_up': w_up.astype(BF16),
        'w_down': w_down.astype(BF16),
    }


PROMPT_TM = 256
FINISH_TM = 512


def kernel(x_prompt, x_sample, cache_ckv, cache_krope, state_ret, page_table, meta_tokens, norm_mix, w_in,
           g_qa, w_qb, g_kva, w_uk, w_uv, g_qn, g_kn, w_ret_o, w_mla_o, w_out, norm_ffn, w_up, w_down):
    nb, seq, _ = x_prompt.shape
    ns = x_sample.shape[0]
    assert x_sample.shape[1] == 1 and norm_mix.shape[0] == 1
    past = page_table.shape[1] * PAGE_SIZE
    w = _layer_weights(norm_mix[0], w_in[0], g_qa[0], w_qb[0], g_kva[0], w_uk[0], w_uv[0], g_qn[0], g_kn[0],
                       w_ret_o[0], w_mla_o[0], w_out[0], norm_ffn[0], w_up[0], w_down[0])
    dtabs = _decay_tables(RET_CHUNK)

    mpad = RET_CHUNK - N_META
    xm = jnp.pad(meta_tokens, ((mpad, 0), (0, 0)))
    pos_m = jnp.maximum(jnp.arange(RET_CHUNK) - mpad, 0)
    rq, rk, rv, rg, _, _, _, km, vm, c_m, kr_m = _proj(xm, _rotary_tables(pos_m), w, RET_CHUNK, BF16)
    zero_state = jnp.zeros((1, RET_HEADS, RET_DK, RET_DV), F32)
    _, s0 = _retention(rq[None], rk[None], rv[None], rg[None], zero_state, dtabs)

    xp = x_prompt.reshape(nb * seq, D_MODEL)
    rq, rk, rv, rg, gr, gm, q, k, v, c_p, kr_p = _proj(
        xp, _rotary_tables(N_META + jnp.arange(seq)), w, PROMPT_TM, BF16)
    b3 = lambda a: a.reshape(nb, seq, a.shape[-1])
    ret_mid, st_p = _retention(b3(rq), b3(rk), b3(rv), b3(rg), s0, dtabs)
    o_mla = _attention(b3(q), b3(k), b3(v), km, vm)
    y_prompt = _finish(xp, ret_mid.reshape(nb * seq, RET_V_W), o_mla.reshape(nb * seq, MLA_V_W), gr, gm, w,
                       FINISH_TM).reshape(nb, seq, D_MODEL)
    bcast = lambda a: jnp.broadcast_to(a[None, mpad:], (nb, N_META, a.shape[-1]))
    ckv_prompt = jnp.concatenate([bcast(c_m), b3(c_p)], axis=1)[None]
    krope_prompt = jnp.concatenate([bcast(kr_m), b3(kr_p)], axis=1)[None]

    xs = x_sample.reshape(ns, D_MODEL)
    pos_s = jnp.full((ns,), past, jnp.int32)
    rq, rk, rv, rg, gr, gm, q, k, _, c_s, kr_s = _proj(xs, _rotary_tables(pos_s), w, ns, F32)
    gam = jnp.exp(jnp.log1p(-jnp.exp2(-5.0 - jnp.arange(RET_HEADS, dtype=F32))))[:, None, None]
    ret_mid, st_s = _ret_decode(rq, rk, rv, rg, gam, state_ret[0])
    qt, qg, s0_new = _absorb(q, k, w['gk'], w['w_ukp_f32'])
    qt = jnp.pad(qt.transpose(1, 0, 2), ((0, 0), (0, QT_ROWS - MLA_HEADS), (0, 0))).astype(BF16)
    q_rope = qg.reshape(ns, MLA_HEADS, HEAD_PAD)[:, :, NOPE_DIM:QK_DIM]
    arope = jnp.concatenate(
        [q_rope, jnp.broadcast_to(jnp.eye(ROPE_DIM, dtype=F32), (ns, ROPE_DIM, ROPE_DIM)),
         jnp.zeros((ns, ROPE_LHS_ROWS - MLA_HEADS - ROPE_DIM, ROPE_DIM), F32)], axis=1).astype(BF16)
    o_mla = _mla_decode(page_table, w['w_uk_t'], qt, arope, s0_new[:, :MLA_HEADS, None], c_s[:, None, :],
                        w['w_uv'], cache_ckv, cache_krope)
    y_sample = _finish(xs, ret_mid, o_mla.reshape(ns, MLA_V_W), gr, gm, w, ns).reshape(ns, 1, D_MODEL)

    return (y_prompt, y_sample, ckv_prompt, krope_prompt, st_p[None],
            c_s.reshape(1, ns, 1, KV_LORA), kr_s.reshape(1, ns, 1, ROPE_DIM), st_s[None])
```

```python
import numpy as np
import jax
import jax.numpy as jnp
from jax import lax
from jax.experimental import pallas as pl
from jax.experimental.pallas import tpu as pltpu

D_MODEL = 1024
N_META = 16
RET_HEADS = 4
RET_DK = 128
RET_DV = 256
RET_CHUNK = 128
MLA_HEADS = 8
Q_LORA = 384
KV_LORA = 256
NOPE_DIM = 64
ROPE_DIM = 32
QK_DIM = NOPE_DIM + ROPE_DIM
V_DIM = 64
D_FF = 4 * D_MODEL
PAGE_SIZE = 128
ROPE_BASE = 10000.0
EPS = 1e-6

RET_QK_W = RET_HEADS * RET_DK
RET_V_W = RET_HEADS * RET_DV
MLA_V_W = MLA_HEADS * V_DIM
SPLITS = (RET_QK_W, RET_QK_W, RET_V_W, RET_V_W, Q_LORA, KV_LORA, ROPE_DIM, D_MODEL, D_MODEL)

LANES = 128
HEAD_PAD = LANES
MLA_PAD_W = MLA_HEADS * HEAD_PAD
_SEG_W = (RET_QK_W, RET_QK_W, RET_V_W, RET_V_W, Q_LORA, KV_LORA, LANES, D_MODEL, D_MODEL)
_SEG_O = tuple(int(v) for v in np.cumsum((0,) + _SEG_W))
IN_PAD_W = _SEG_O[-1]
VMEM_LIMIT = 56 * 1024 * 1024

F32 = jnp.float32
BF16 = jnp.bfloat16
_NT = (((1,), (1,)), ((), ()))
_TN = (((0,), (0,)), ((), ()))


def _dot(a, b):
    return jnp.dot(a, b, preferred_element_type=F32)


def _dot_nt(a, b):
    return lax.dot_general(a, b, _NT, preferred_element_type=F32)


def _const_spec(shape):
    zeros = (0,) * len(shape)
    return pl.BlockSpec(shape, lambda *_: zeros, pipeline_mode=pl.Buffered(1))


def _params(sem):
    return pltpu.CompilerParams(dimension_semantics=sem, vmem_limit_bytes=VMEM_LIMIT)


def _proj_kernel(x_ref, cosr_ref, sinr_ref, cm_ref, sa_ref, sb_ref, nmix_ref, win_ref, gqa_ref,
                 wqb_ref, gkva_ref, wukp_ref, wuvt_ref, gq_ref, gk_ref,
                 rq_ref, rk_ref, rv_ref, rg_ref, gr_ref, gm_ref, q_ref, k_ref, vt_ref, c_ref, kr_ref):
    x = x_ref[...]
    xn = x * lax.rsqrt(jnp.mean(x * x, axis=-1, keepdims=True) + EPS) * nmix_ref[...]
    xb = xn.astype(BF16)

    def seg(i):
        return _dot(xb, win_ref[:, _SEG_O[i]:_SEG_O[i + 1]])

    cosr, sinr = cosr_ref[...], sinr_ref[...]
    cm, sa, sb = cm_ref[...], sa_ref[...], sb_ref[...]

    def ret_rotary(z, out_ref, scale):
        for h in range(RET_HEADS):
            zh = z[:, h * RET_DK:(h + 1) * RET_DK]
            r = zh * cosr + pltpu.roll(zh, RET_DK // 2, 1) * sinr
            if scale is not None:
                r = r * scale
            out_ref[:, h * RET_DK:(h + 1) * RET_DK] = r.astype(out_ref.dtype)

    def mla_rotary(zh):
        half = ROPE_DIM // 2
        return zh * cm + pltpu.roll(zh, HEAD_PAD - half, 1) * sa + pltpu.roll(zh, half, 1) * sb

    def head_norm(zh, g):
        ss = jnp.sum(zh * zh, axis=-1, keepdims=True)
        return zh * lax.rsqrt(ss * (1.0 / QK_DIM) + EPS) * g

    ret_rotary(seg(0), rq_ref, None)
    ret_rotary(seg(1), rk_ref, RET_DK ** -0.5)
    rv_ref[...] = seg(2).astype(rv_ref.dtype)
    rg_ref[...] = seg(3).astype(rg_ref.dtype)
    gr_ref[...] = seg(7).astype(gr_ref.dtype)
    gm_ref[...] = seg(8).astype(gm_ref.dtype)

    qa = seg(4)
    qn = qa * lax.rsqrt(jnp.mean(qa * qa, axis=-1, keepdims=True) + EPS) * gqa_ref[...]
    qp = _dot(qn.astype(BF16), wqb_ref[...])
    gq = gq_ref[...]
    for h in range(MLA_HEADS):
        qh = mla_rotary(qp[:, h * HEAD_PAD:(h + 1) * HEAD_PAD])
        q_ref[:, h * HEAD_PAD:(h + 1) * HEAD_PAD] = head_norm(qh, gq).astype(q_ref.dtype)

    ckv = seg(5)
    c = ckv * lax.rsqrt(jnp.mean(ckv * ckv, axis=-1, keepdims=True) + EPS) * gkva_ref[...]
    c_ref[...] = c
    cb = c.astype(BF16)
    krp = mla_rotary(seg(6))
    kr_ref[...] = krp[:, NOPE_DIM:QK_DIM]
    kp = _dot(cb, wukp_ref[...])
    gk = gk_ref[...]
    for h in range(MLA_HEADS):
        kh = kp[:, h * HEAD_PAD:(h + 1) * HEAD_PAD] + krp
        k_ref[:, h * HEAD_PAD:(h + 1) * HEAD_PAD] = head_norm(kh, gk).astype(k_ref.dtype)
    vt_ref[...] = _dot_nt(wuvt_ref[...], cb).astype(vt_ref.dtype)


def _proj(x, tabs, w, tm, mid_dtype):
    rows = x.shape[0]
    period = tabs[0].shape[0] // tm
    row = lambda i: (i, 0)
    tab_spec = pl.BlockSpec((tm, LANES), lambda i: (i % period, 0))
    widths = (RET_QK_W, RET_QK_W, RET_V_W, RET_V_W, D_MODEL, D_MODEL, MLA_PAD_W, MLA_PAD_W)
    out_shape = [jax.ShapeDtypeStruct((rows, n), mid_dtype) for n in widths]
    out_shape += [jax.ShapeDtypeStruct((rows // tm, MLA_V_W, tm), mid_dtype),
                  jax.ShapeDtypeStruct((rows, KV_LORA), F32), jax.ShapeDtypeStruct((rows, ROPE_DIM), F32)]
    out_specs = [pl.BlockSpec((tm, n), row) for n in widths]
    out_specs += [pl.BlockSpec((None, MLA_V_W, tm), lambda i: (i, 0, 0)),
                  pl.BlockSpec((tm, KV_LORA), row), pl.BlockSpec((tm, ROPE_DIM), row)]
    consts = (w['norm_mix'], w['w_in'], w['g_qa'], w['w_qb'], w['g_kva'], w['w_ukp'], w['w_uv_t'],
              w['gq'], w['gk'])
    return pl.pallas_call(
        _proj_kernel,
        grid=(rows // tm,),
        in_specs=[pl.BlockSpec((tm, D_MODEL), row)] + [tab_spec] * 5 + [_const_spec(a.shape) for a in consts],
        out_specs=out_specs,
        out_shape=out_shape,
        compiler_params=_params(("parallel",)),
        name="proj",
    )(x, *tabs, *consts)


def _retention_kernel(q_ref, k_ref, v_ref, rg_ref, dec_ref, qdec_ref, kdec_ref, sdec_ref, s0_ref,
                      o_ref, st_ref, s_scr):
    ci = pl.program_id(1)

    @pl.when(ci == 0)
    def _():
        s_scr[...] = s0_ref[0]

    for h in range(RET_HEADS):
        qh = q_ref[:, h * RET_DK:(h + 1) * RET_DK]
        kh = k_ref[:, h * RET_DK:(h + 1) * RET_DK]
        vh = v_ref[:, h * RET_DV:(h + 1) * RET_DV]
        scores = _dot_nt(qh, kh) * dec_ref[h]
        o = _dot(scores.astype(BF16), vh)
        s_prev = s_scr[h]
        o = o + _dot(qh, s_prev.astype(BF16)) * qdec_ref[h]
        kd = (kh.astype(F32) * kdec_ref[h]).astype(BF16)
        s_scr[h] = sdec_ref[h] * s_prev + lax.dot_general(kd, vh, _TN, preferred_element_type=F32)
        on = o * lax.rsqrt(jnp.mean(o * o, axis=-1, keepdims=True) + EPS)
        g = rg_ref[:, h * RET_DV:(h + 1) * RET_DV].astype(F32)
        o_ref[:, h * RET_DV:(h + 1) * RET_DV] = (g * jax.nn.sigmoid(g) * on).astype(o_ref.dtype)

    @pl.when(ci == pl.num_programs(1) - 1)
    def _():
        st_ref[0] = s_scr[...]


def _retention(rq, rk, rv, rg, s0, dtabs):
    nb, s = rq.shape[:2]
    c = RET_CHUNK
    blk = lambda n: pl.BlockSpec((None, c, n), lambda b, i: (b, i, 0))
    st_shape = (1, RET_HEADS, RET_DK, RET_DV)
    return pl.pallas_call(
        _retention_kernel,
        grid=(nb, s // c),
        in_specs=[blk(RET_QK_W), blk(RET_QK_W), blk(RET_V_W), blk(RET_V_W)]
        + [_const_spec(t.shape) for t in dtabs] + [_const_spec(st_shape)],
        out_specs=[blk(RET_V_W), pl.BlockSpec(st_shape, lambda b, i: (b, 0, 0, 0))],
        out_shape=[jax.ShapeDtypeStruct((nb, s, RET_V_W), BF16),
                   jax.ShapeDtypeStruct((nb,) + st_shape[1:], F32)],
        scratch_shapes=[pltpu.VMEM(st_shape[1:], F32)],
        compiler_params=_params(("parallel", "arbitrary")),
        name="retention",
    )(rq, rk, rv, rg, *dtabs, s0)


ATT_T = 256


def _attention_kernel(q_ref, k_ref, vt_ref, km_ref, vmt_ref, o_ref, m_scr, l_scr, acc_scr):
    qi = pl.program_id(1)
    meta_rows = km_ref.shape[0]
    hs = lambda h: slice(h * HEAD_PAD, (h + 1) * HEAD_PAD)
    vs = lambda h: slice(h * V_DIM, (h + 1) * V_DIM)

    meta_ok = lax.broadcasted_iota(jnp.int32, (meta_rows, ATT_T), 0) >= meta_rows - N_META
    for h in range(MLA_HEADS):
        s = jnp.where(meta_ok, _dot_nt(km_ref[:, hs(h)], q_ref[:, hs(h)]), -jnp.inf)
        m = jnp.max(s, axis=0, keepdims=True)
        p = jnp.exp(s - m)
        m_scr[h:h + 1, :] = m
        l_scr[h:h + 1, :] = jnp.sum(p, axis=0, keepdims=True)
        acc_scr[vs(h), :] = _dot(vmt_ref[vs(h), :], p.astype(BF16))

    causal = (lax.broadcasted_iota(jnp.int32, (ATT_T, ATT_T), 0)
              <= lax.broadcasted_iota(jnp.int32, (ATT_T, ATT_T), 1))

    def update(j, masked):
        start = pl.multiple_of(j * ATT_T, ATT_T)
        for h in range(MLA_HEADS):
            s = _dot_nt(k_ref[pl.ds(start, ATT_T), hs(h)], q_ref[:, hs(h)])
            if masked:
                s = jnp.where(causal, s, -jnp.inf)
            m_old = m_scr[h:h + 1, :]
            m_new = jnp.maximum(m_old, jnp.max(s, axis=0, keepdims=True))
            alpha = jnp.exp(m_old - m_new)
            p = jnp.exp(s - m_new)
            m_scr[h:h + 1, :] = m_new
            l_scr[h:h + 1, :] = alpha * l_scr[h:h + 1, :] + jnp.sum(p, axis=0, keepdims=True)
            acc_scr[vs(h), :] = alpha * acc_scr[vs(h), :] + _dot(vt_ref[j, vs(h), :], p.astype(BF16))

    def body(j, carry):
        update(j, False)
        return carry

    lax.fori_loop(0, qi, body, 0)
    update(qi, True)
    for h in range(MLA_HEADS):
        acc_scr[vs(h), :] = acc_scr[vs(h), :] / l_scr[h:h + 1, :]
    o_ref[...] = acc_scr[...].T.astype(o_ref.dtype)


def _attention(q, k, vt, km, vmt):
    nb, s = q.shape[:2]
    return pl.pallas_call(
        _attention_kernel,
        grid=(nb, s // ATT_T),
        in_specs=[pl.BlockSpec((None, ATT_T, MLA_PAD_W), lambda b, i: (b, i, 0)),
                  pl.BlockSpec((None, s, MLA_PAD_W), lambda b, i: (b, 0, 0)),
                  pl.BlockSpec((None, s // ATT_T, MLA_V_W, ATT_T), lambda b, i: (b, 0, 0, 0)),
                  _const_spec(km.shape), _const_spec(vmt.shape)],
        out_specs=pl.BlockSpec((None, ATT_T, MLA_V_W), lambda b, i: (b, i, 0)),
        out_shape=jax.ShapeDtypeStruct((nb, s, MLA_V_W), BF16),
        scratch_shapes=[pltpu.VMEM((MLA_HEADS, ATT_T), F32), pltpu.VMEM((MLA_HEADS, ATT_T), F32),
                        pltpu.VMEM((MLA_V_W, ATT_T), F32)],
        compiler_params=_params(("parallel", "arbitrary")),
        name="attention",
    )(q, k, vt, km, vmt)


FF_BLOCK = 1024


def _finish_kernel(x_ref, ret_ref, mla_ref, gr_ref, gm_ref, wro_ref, wmo_ref, wout_ref, nffn_ref,
                   wup_ref, wdn_ref, y_ref):
    r = _dot(ret_ref[...].astype(BF16), wro_ref[...])
    m = _dot(mla_ref[...].astype(BF16), wmo_ref[...])
    mixed = jax.nn.sigmoid(gr_ref[...].astype(F32)) * r + jax.nn.sigmoid(gm_ref[...].astype(F32)) * m
    h = x_ref[...] + _dot(mixed.astype(BF16), wout_ref[...])
    hn = h * lax.rsqrt(jnp.mean(h * h, axis=-1, keepdims=True) + EPS) * nffn_ref[...]
    hb = hn.astype(BF16)
    y = h
    for f in range(D_FF // FF_BLOCK):
        u = jnp.maximum(_dot(hb, wup_ref[:, f * FF_BLOCK:(f + 1) * FF_BLOCK]), 0.0)
        y = y + _dot((u * u).astype(BF16), wdn_ref[f * FF_BLOCK:(f + 1) * FF_BLOCK, :])
    y_ref[...] = y


def _finish(x, ret_mid, o_mla, gr, gm, w, tm):
    rows = x.shape[0]
    row = lambda i: (i, 0)
    consts = (w['w_ret_o'], w['w_mla_o'], w['w_out'], w['norm_ffn'], w['w_up'], w['w_down'])
    return pl.pallas_call(
        _finish_kernel,
        grid=(rows // tm,),
        in_specs=[pl.BlockSpec((tm, D_MODEL), row), pl.BlockSpec((tm, RET_V_W), row),
                  pl.BlockSpec((tm, MLA_V_W), row), pl.BlockSpec((tm, D_MODEL), row),
                  pl.BlockSpec((tm, D_MODEL), row)] + [_const_spec(a.shape) for a in consts],
        out_specs=pl.BlockSpec((tm, D_MODEL), row),
        out_shape=jax.ShapeDtypeStruct((rows, D_MODEL), F32),
        compiler_params=_params(("parallel",)),
        name="finish",
    )(x, ret_mid, o_mla, gr, gm, *consts)


RET_DEC_GROUP = 8


def _column(row_vec, eye):
    return jnp.sum(jnp.where(eye, row_vec, 0.0), axis=-1, keepdims=True)


def _ret_decode_kernel(q_ref, k_ref, v_ref, rg_ref, gam_ref, s_ref, o_ref, sn_ref):
    eye = (lax.broadcasted_iota(jnp.int32, (RET_DK, RET_DK), 0)
           == lax.broadcasted_iota(jnp.int32, (RET_DK, RET_DK), 1))
    for b in range(RET_DEC_GROUP):
        for h in range(RET_HEADS):
            k_col = _column(k_ref[b:b + 1, h * RET_DK:(h + 1) * RET_DK].astype(F32), eye)
            q_col = _column(q_ref[b:b + 1, h * RET_DK:(h + 1) * RET_DK].astype(F32), eye)
            v_row = v_ref[b:b + 1, h * RET_DV:(h + 1) * RET_DV].astype(F32)
            gam = gam_ref[h]
            s_prev = s_ref[b, h]
            kv = k_col * v_row
            sn_ref[b, h] = gam * s_prev + kv
            o = jnp.sum(q_col * kv, axis=0, keepdims=True) + gam * jnp.sum(q_col * s_prev, axis=0, keepdims=True)
            on = o * lax.rsqrt(jnp.mean(o * o, axis=-1, keepdims=True) + EPS)
            g = rg_ref[b:b + 1, h * RET_DV:(h + 1) * RET_DV].astype(F32)
            o_ref[b:b + 1, h * RET_DV:(h + 1) * RET_DV] = (g * jax.nn.sigmoid(g) * on).astype(o_ref.dtype)


def _ret_decode(rq, rk, rv, rg, gam, state):
    nb = rq.shape[0]
    g = RET_DEC_GROUP
    row = lambda i: (i, 0)
    st_spec = pl.BlockSpec((g, RET_HEADS, RET_DK, RET_DV), lambda i: (i, 0, 0, 0))
    return pl.pallas_call(
        _ret_decode_kernel,
        grid=(nb // g,),
        in_specs=[pl.BlockSpec((g, RET_QK_W), row), pl.BlockSpec((g, RET_QK_W), row),
                  pl.BlockSpec((g, RET_V_W), row), pl.BlockSpec((g, RET_V_W), row),
                  _const_spec(gam.shape), st_spec],
        out_specs=[pl.BlockSpec((g, RET_V_W), row), st_spec],
        out_shape=[jax.ShapeDtypeStruct((nb, RET_V_W), F32), jax.ShapeDtypeStruct(state.shape, F32)],
        compiler_params=_params(("parallel",)),
        name="ret_decode",
    )(rq, rk, rv, rg, gam, state)


def _absorb_kernel(q_ref, k_ref, gk_ref, wukp_ref, qt_ref, qg_ref, s0_ref):
    q = q_ref[...]
    lane = lax.broadcasted_iota(jnp.int32, (q.shape[0], LANES), 1)
    s0 = jnp.zeros((q.shape[0], LANES), F32)
    gk = gk_ref[...]
    for h in range(MLA_HEADS):
        sl = slice(h * HEAD_PAD, (h + 1) * HEAD_PAD)
        qh = q[:, sl]
        s0 = s0 + jnp.where(lane == h, jnp.sum(qh * k_ref[:, sl], axis=-1, keepdims=True), 0.0)
        qg = qh * gk
        qg_ref[:, sl] = qg
        qt_ref[h] = lax.dot_general(qg, wukp_ref[:, sl], _NT, precision=lax.Precision.HIGHEST,
                                    preferred_element_type=F32)
    s0_ref[...] = s0


def _absorb(q, k, gk, wukp):
    nb = q.shape[0]
    return pl.pallas_call(
        _absorb_kernel,
        out_shape=[jax.ShapeDtypeStruct((MLA_HEADS, nb, KV_LORA), F32),
                   jax.ShapeDtypeStruct((nb, MLA_PAD_W), F32),
                   jax.ShapeDtypeStruct((nb, LANES), F32)],
        compiler_params=pltpu.CompilerParams(vmem_limit_bytes=VMEM_LIMIT),
        name="absorb",
    )(q, k, gk, wukp)


DEC_PAGES = 16
DEC_T = DEC_PAGES * PAGE_SIZE
DEC_SUB = 256
UK_ROWS = MLA_HEADS * NOPE_DIM
QT_ROWS = 16


def _mla_decode_kernel(pt_ref, lhsw_ref, qt_ref, qrope_ref, s0_ref, cnew_ref, wuv_ref, ckv_hbm, krt_hbm,
                       o_ref, lhs_scr, cbuf, kbuf, cb_scr, sem):
    b = pl.program_id(0)
    nb = pl.num_programs(0)
    n_groups = pt_ref.shape[1] // DEC_PAGES

    def page_copies(seq, g, slot):
        copies = []
        for i in range(DEC_PAGES):
            page = pt_ref[seq, g * DEC_PAGES + i]
            tok = pl.ds(i * PAGE_SIZE, PAGE_SIZE)
            copies.append(pltpu.make_async_copy(ckv_hbm.at[0, page], cbuf.at[slot, tok], sem.at[0, slot]))
            copies.append(pltpu.make_async_copy(krt_hbm.at[0, page], kbuf.at[slot, :, tok], sem.at[1, slot]))
        return copies

    def start_group(seq, g, slot):
        for cp in page_copies(seq, g, slot):
            cp.start()

    def wait_group(slot):
        for cp in page_copies(0, 0, slot):
            cp.wait()

    def scores(slot):
        pieces = []
        for t in range(DEC_T // DEC_SUB):
            tok = pl.ds(t * DEC_SUB, DEC_SUB)
            cb = cbuf[slot, tok, :].astype(BF16)
            cb_scr[slot, tok, :] = cb
            big = _dot_nt(lhs_scr[...], cb)
            sq = [big[d * MLA_HEADS:(d + 1) * MLA_HEADS] for d in range(NOPE_DIM)]
            sq = [x * x for x in sq]
            while len(sq) > 1:
                sq = [sq[i] + sq[i + 1] for i in range(0, len(sq), 2)]
            krt = kbuf[slot, :, tok]
            kk = krt * krt
            kk = (kk[0:8] + kk[8:16]) + (kk[16:24] + kk[24:32])
            ss = sq[0] + jnp.sum(kk, axis=0, keepdims=True)
            num = big[UK_ROWS:UK_ROWS + MLA_HEADS] + _dot(qrope_ref[0], krt.astype(BF16))[0:MLA_HEADS]
            pieces.append(num * lax.rsqrt(ss * (1.0 / QK_DIM) + EPS))
        return jnp.concatenate(pieces, axis=-1)

    def softmax_update(s, slot, m, l, acc):
        m_new = jnp.maximum(m, jnp.max(s, axis=-1, keepdims=True))
        alpha = jnp.exp(m - m_new)
        p = jnp.exp(s - m_new)
        l = alpha * l + jnp.sum(p, axis=-1, keepdims=True)
        acc = alpha * acc + _dot(p.astype(BF16), cb_scr[slot])
        return m_new, l, acc

    @pl.when(b == 0)
    def _():
        start_group(0, 0, 0)

    lhs_scr[0:UK_ROWS, :] = lhsw_ref[...]
    lhs_scr[UK_ROWS:UK_ROWS + QT_ROWS, :] = qt_ref[0]

    wait_group(0)
    start_group(b, 1, 1)
    s_first = scores(0)

    def body(g, carry):
        s_prev, m, l, acc = carry
        slot = g % 2
        wait_group(slot)

        @pl.when(g + 1 < n_groups)
        def _():
            start_group(b, g + 1, 1 - slot)

        @pl.when(jnp.logical_and(g + 1 == n_groups, b + 1 < nb))
        def _():
            start_group(b + 1, 0, 1 - slot)

        s_cur = scores(slot)
        m, l, acc = softmax_update(s_prev, 1 - slot, m, l, acc)
        return s_cur, m, l, acc

    init = (s_first, s0_ref[0], jnp.ones((MLA_HEADS, 1), F32),
            jnp.broadcast_to(cnew_ref[0], (MLA_HEADS, KV_LORA)))
    s_last, m, l, acc = lax.fori_loop(1, n_groups, body, init)
    m, l, acc = softmax_update(s_last, (n_groups - 1) % 2, m, l, acc)
    lat = acc / l
    full = _dot(lat.astype(BF16), wuv_ref[...])
    own = (lax.broadcasted_iota(jnp.int32, full.shape, 1) // V_DIM
           == lax.broadcasted_iota(jnp.int32, full.shape, 0))
    o_ref[0] = jnp.sum(jnp.where(own, full, 0.0), axis=0, keepdims=True)


def _mla_decode(page_table, lhsw, qt, qrope, s0, c_new, wuv, cache_ckv, cache_krt):
    nb, n_pages = page_table.shape
    assert n_pages % (2 * DEC_PAGES) == 0
    per_seq = lambda shape: pl.BlockSpec((1,) + shape, lambda i, pt: (i, 0, 0))
    const = lambda a: pl.BlockSpec(a.shape, lambda i, pt: (0,) * a.ndim, pipeline_mode=pl.Buffered(1))
    grid_spec = pltpu.PrefetchScalarGridSpec(
        num_scalar_prefetch=1,
        grid=(nb,),
        in_specs=[const(lhsw), per_seq((QT_ROWS, KV_LORA)), per_seq((QT_ROWS, ROPE_DIM)),
                  per_seq((MLA_HEADS, 1)), per_seq((1, KV_LORA)), const(wuv),
                  pl.BlockSpec(memory_space=pl.ANY), pl.BlockSpec(memory_space=pl.ANY)],
        out_specs=per_seq((1, MLA_V_W)),
        scratch_shapes=[pltpu.VMEM((UK_ROWS + QT_ROWS, KV_LORA), BF16),
                        pltpu.VMEM((2, DEC_T, KV_LORA), F32),
                        pltpu.VMEM((2, ROPE_DIM, DEC_T), F32),
                        pltpu.VMEM((2, DEC_T, KV_LORA), BF16),
                        pltpu.SemaphoreType.DMA((2, 2))],
    )
    return pl.pallas_call(
        _mla_decode_kernel,
        grid_spec=grid_spec,
        out_shape=jax.ShapeDtypeStruct((nb, 1, MLA_V_W), F32),
        compiler_params=_params(("arbitrary",)),
        name="mla_decode",
    )(page_table, lhsw, qt, qrope, s0, c_new, wuv, cache_ckv, cache_krt)


def _rotary_tables(pos):
    pos = pos.astype(F32)[:, None]

    def cos_sin(half):
        inv = ROPE_BASE ** (-jnp.arange(half, dtype=F32) / half)
        ang = pos * inv[None, :]
        return jnp.cos(ang), jnp.sin(ang)

    cr, sr = cos_sin(RET_DK // 2)
    cosr = jnp.concatenate([cr, cr], axis=-1)
    sinr = jnp.concatenate([-sr, sr], axis=-1)
    c, s = cos_sin(ROPE_DIM // 2)
    n = pos.shape[0]
    half = ROPE_DIM // 2
    z = lambda w: jnp.zeros((n, w), F32)
    cm = jnp.concatenate([jnp.ones((n, NOPE_DIM), F32), c, c, z(HEAD_PAD - QK_DIM)], axis=-1)
    sa = jnp.concatenate([z(NOPE_DIM), -s, z(half), z(HEAD_PAD - QK_DIM)], axis=-1)
    sb = jnp.concatenate([z(NOPE_DIM), z(half), s, z(HEAD_PAD - QK_DIM)], axis=-1)
    return cosr, sinr, cm, sa, sb


def _decay_tables(c):
    lg = jnp.log1p(-jnp.exp2(-5.0 - jnp.arange(RET_HEADS, dtype=F32)))
    idx = jnp.arange(c, dtype=F32)
    diff = idx[:, None] - idx[None, :]
    dec = jnp.where(diff[None] >= 0, jnp.exp(jnp.maximum(diff, 0.0)[None] * lg[:, None, None]), 0.0)
    qdec = jnp.exp((idx + 1.0)[None, :] * lg[:, None])[:, :, None]
    kdec = jnp.exp((c - 1.0 - idx)[None, :] * lg[:, None])[:, :, None]
    sdec = jnp.exp(c * lg)[:, None, None]
    return dec, qdec, kdec, sdec


def _pad_heads(a):
    a = jnp.pad(a, [(0, 0)] * (a.ndim - 1) + [(0, HEAD_PAD - a.shape[-1])])
    return a.reshape(a.shape[:-2] + (a.shape[-2] * HEAD_PAD,))


def _layer_weights(norm_mix, w_in, g_qa, w_qb, g_kva, w_uk, w_uv, g_qn, g_kn, w_ret_o, w_mla_o, w_out,
                   norm_ffn, w_up, w_down):
    idx = [int(i) for i in np.cumsum(SPLITS)[:-1]]
    parts = jnp.split(w_in, idx, axis=-1)
    parts[6] = jnp.pad(parts[6], ((0, 0), (NOPE_DIM, HEAD_PAD - QK_DIM)))
    pad1 = lambda g: jnp.pad(g, (0, HEAD_PAD - QK_DIM))[None, :]
    wukp = _pad_heads(w_uk)
    return {
        'norm_mix': norm_mix[None, :],
        'w_in': jnp.concatenate(parts, axis=-1).astype(BF16),
        'g_qa': g_qa[None, :],
        'w_qb': _pad_heads(w_qb.reshape(Q_LORA, MLA_HEADS, QK_DIM)).astype(BF16),
        'g_kva': g_kva[None, :],
        'w_ukp': wukp.astype(BF16),
        'w_ukp_f32': wukp,
        'w_uk_t': w_uk.transpose(2, 1, 0).reshape(UK_ROWS, KV_LORA).astype(BF16),
        'w_uv': w_uv.reshape(KV_LORA, MLA_V_W).astype(BF16),
        'w_uv_t': w_uv.reshape(KV_LORA, MLA_V_W).T.astype(BF16),
        'gq': pad1(g_qn) * (QK_DIM ** -0.5),
        'gk': pad1(g_kn),
        'w_ret_o': w_ret_o.astype(BF16),
        'w_mla_o': w_mla_o.astype(BF16),
        'w_out': w_out.astype(BF16),
        'norm_ffn': norm_ffn[None, :],
        'w_up': w_up.astype(BF16),
        'w_down': w_down.astype(BF16),
    }


FINISH_TM = 512


def kernel(x_prompt, x_sample, cache_ckv, cache_krope, state_ret, page_table, meta_tokens, norm_mix, w_in,
           g_qa, w_qb, g_kva, w_uk, w_uv, g_qn, g_kn, w_ret_o, w_mla_o, w_out, norm_ffn, w_up, w_down):
    nb, seq, _ = x_prompt.shape
    ns = x_sample.shape[0]
    assert x_sample.shape[1] == 1 and norm_mix.shape[0] == 1
    past = page_table.shape[1] * PAGE_SIZE
    w = _layer_weights(norm_mix[0], w_in[0], g_qa[0], w_qb[0], g_kva[0], w_uk[0], w_uv[0], g_qn[0], g_kn[0],
                       w_ret_o[0], w_mla_o[0], w_out[0], norm_ffn[0], w_up[0], w_down[0])
    dtabs = _decay_tables(RET_CHUNK)

    mpad = RET_CHUNK - N_META
    xm = jnp.pad(meta_tokens, ((mpad, 0), (0, 0)))
    pos_m = jnp.maximum(jnp.arange(RET_CHUNK) - mpad, 0)
    rq, rk, rv, rg, _, _, _, km, vmt, c_m, kr_m = _proj(xm, _rotary_tables(pos_m), w, RET_CHUNK, BF16)
    zero_state = jnp.zeros((1, RET_HEADS, RET_DK, RET_DV), F32)
    _, s0 = _retention(rq[None], rk[None], rv[None], rg[None], zero_state, dtabs)

    xp = x_prompt.reshape(nb * seq, D_MODEL)
    rq, rk, rv, rg, gr, gm, q, k, vt, c_p, kr_p = _proj(
        xp, _rotary_tables(N_META + jnp.arange(seq)), w, ATT_T, BF16)
    b3 = lambda a: a.reshape(nb, seq, a.shape[-1])
    ret_mid, st_p = _retention(b3(rq), b3(rk), b3(rv), b3(rg), s0, dtabs)
    o_mla = _attention(b3(q), b3(k), vt.reshape(nb, seq // ATT_T, MLA_V_W, ATT_T), km, vmt[0])
    y_prompt = _finish(xp, ret_mid.reshape(nb * seq, RET_V_W), o_mla.reshape(nb * seq, MLA_V_W), gr, gm, w,
                       FINISH_TM).reshape(nb, seq, D_MODEL)
    bcast = lambda a: jnp.broadcast_to(a[None, mpad:], (nb, N_META, a.shape[-1]))
    ckv_prompt = jnp.concatenate([bcast(c_m), b3(c_p)], axis=1)[None]
    krope_prompt = jnp.concatenate([bcast(kr_m), b3(kr_p)], axis=1)[None]

    xs = x_sample.reshape(ns, D_MODEL)
    pos_s = jnp.full((ns,), past, jnp.int32)
    rq, rk, rv, rg, gr, gm, q, k, _, c_s, kr_s = _proj(xs, _rotary_tables(pos_s), w, ns, F32)
    gam = jnp.exp(jnp.log1p(-jnp.exp2(-5.0 - jnp.arange(RET_HEADS, dtype=F32))))[:, None, None]
    ret_mid, st_s = _ret_decode(rq, rk, rv, rg, gam, state_ret[0])
    qt, qg, s0_new = _absorb(q, k, w['gk'], w['w_ukp_f32'])
    qt = jnp.pad(qt.transpose(1, 0, 2), ((0, 0), (0, QT_ROWS - MLA_HEADS), (0, 0))).astype(BF16)
    q_rope = qg.reshape(ns, MLA_HEADS, HEAD_PAD)[:, :, NOPE_DIM:QK_DIM]
    q_rope = jnp.pad(q_rope, ((0, 0), (0, QT_ROWS - MLA_HEADS), (0, 0))).astype(BF16)
    o_mla = _mla_decode(page_table, w['w_uk_t'], qt, q_rope, s0_new[:, :MLA_HEADS, None], c_s[:, None, :],
                        w['w_uv'], cache_ckv, jnp.swapaxes(cache_krope, 2, 3))
    y_sample = _finish(xs, ret_mid, o_mla.reshape(ns, MLA_V_W), gr, gm, w, ns).reshape(ns, 1, D_MODEL)

    return (y_prompt, y_sample, ckv_prompt, krope_prompt, st_p[None],
            c_s.reshape(1, ns, 1, KV_LORA), kr_s.reshape(1, ns, 1, ROPE_DIM), st_s[None])
```

```python
import numpy as np
import jax
import jax.numpy as jnp
from jax import lax
from jax.experimental import pallas as pl
from jax.experimental.pallas import tpu as pltpu

D_MODEL = 1024
N_META = 16
RET_HEADS = 4
RET_DK = 128
RET_DV = 256
RET_CHUNK = 128
MLA_HEADS = 8
Q_LORA = 384
KV_LORA = 256
NOPE_DIM = 64
ROPE_DIM = 32
QK_DIM = NOPE_DIM + ROPE_DIM
V_DIM = 64
D_FF = 4 * D_MODEL
PAGE_SIZE = 128
ROPE_BASE = 10000.0
EPS = 1e-6

RET_QK_W = RET_HEADS * RET_DK
RET_V_W = RET_HEADS * RET_DV
MLA_V_W = MLA_HEADS * V_DIM
SPLITS = (RET_QK_W, RET_QK_W, RET_V_W, RET_V_W, Q_LORA, KV_LORA, ROPE_DIM, D_MODEL, D_MODEL)

LANES = 128
HEAD_PAD = LANES
MLA_PAD_W = MLA_HEADS * HEAD_PAD
_SEG_W = (RET_QK_W, RET_QK_W, RET_V_W, RET_V_W, Q_LORA, KV_LORA, LANES, D_MODEL, D_MODEL)
_SEG_O = tuple(int(v) for v in np.cumsum((0,) + _SEG_W))
IN_PAD_W = _SEG_O[-1]
VMEM_LIMIT = 56 * 1024 * 1024

F32 = jnp.float32
BF16 = jnp.bfloat16
_NT = (((1,), (1,)), ((), ()))
_TN = (((0,), (0,)), ((), ()))


def _dot(a, b):
    return jnp.dot(a, b, preferred_element_type=F32)


def _dot_nt(a, b):
    return lax.dot_general(a, b, _NT, preferred_element_type=F32)


def _const_spec(shape):
    zeros = (0,) * len(shape)
    return pl.BlockSpec(shape, lambda *_: zeros, pipeline_mode=pl.Buffered(1))


def _params(sem):
    return pltpu.CompilerParams(dimension_semantics=sem, vmem_limit_bytes=VMEM_LIMIT)


def _proj_kernel(x_ref, cosr_ref, sinr_ref, cm_ref, sa_ref, sb_ref, nmix_ref, win_ref, gqa_ref,
                 wqb_ref, gkva_ref, wukp_ref, wuvt_ref, gq_ref, gk_ref,
                 rq_ref, rk_ref, rv_ref, rg_ref, gr_ref, gm_ref, q_ref, k_ref, vt_ref, c_ref, kr_ref):
    x = x_ref[...]
    xn = x * lax.rsqrt(jnp.mean(x * x, axis=-1, keepdims=True) + EPS) * nmix_ref[...]
    xb = xn.astype(BF16)

    def seg(i):
        return _dot(xb, win_ref[:, _SEG_O[i]:_SEG_O[i + 1]])

    cosr, sinr = cosr_ref[...], sinr_ref[...]
    cm, sa, sb = cm_ref[...], sa_ref[...], sb_ref[...]

    def ret_rotary(z, out_ref, scale):
        for h in range(RET_HEADS):
            zh = z[:, h * RET_DK:(h + 1) * RET_DK]
            r = zh * cosr + pltpu.roll(zh, RET_DK // 2, 1) * sinr
            if scale is not None:
                r = r * scale
            out_ref[:, h * RET_DK:(h + 1) * RET_DK] = r.astype(out_ref.dtype)

    def mla_rotary(zh):
        half = ROPE_DIM // 2
        return zh * cm + pltpu.roll(zh, HEAD_PAD - half, 1) * sa + pltpu.roll(zh, half, 1) * sb

    def head_norm(zh, g):
        ss = jnp.sum(zh * zh, axis=-1, keepdims=True)
        return zh * lax.rsqrt(ss * (1.0 / QK_DIM) + EPS) * g

    ret_rotary(seg(0), rq_ref, None)
    ret_rotary(seg(1), rk_ref, RET_DK ** -0.5)
    rv_ref[...] = seg(2).astype(rv_ref.dtype)
    rg_ref[...] = seg(3).astype(rg_ref.dtype)
    gr_ref[...] = seg(7).astype(gr_ref.dtype)
    gm_ref[...] = seg(8).astype(gm_ref.dtype)

    qa = seg(4)
    qn = qa * lax.rsqrt(jnp.mean(qa * qa, axis=-1, keepdims=True) + EPS) * gqa_ref[...]
    qp = _dot(qn.astype(BF16), wqb_ref[...])
    gq = gq_ref[...]
    for h in range(MLA_HEADS):
        qh = mla_rotary(qp[:, h * HEAD_PAD:(h + 1) * HEAD_PAD])
        q_ref[:, h * HEAD_PAD:(h + 1) * HEAD_PAD] = head_norm(qh, gq).astype(q_ref.dtype)

    ckv = seg(5)
    c = ckv * lax.rsqrt(jnp.mean(ckv * ckv, axis=-1, keepdims=True) + EPS) * gkva_ref[...]
    c_ref[...] = c
    cb = c.astype(BF16)
    krp = mla_rotary(seg(6))
    kr_ref[...] = krp[:, NOPE_DIM:QK_DIM]
    kp = _dot(cb, wukp_ref[...])
    gk = gk_ref[...]
    for h in range(MLA_HEADS):
        kh = kp[:, h * HEAD_PAD:(h + 1) * HEAD_PAD] + krp
        k_ref[:, h * HEAD_PAD:(h + 1) * HEAD_PAD] = head_norm(kh, gk).astype(k_ref.dtype)
    vt_ref[...] = _dot_nt(wuvt_ref[...], cb).astype(vt_ref.dtype)


def _proj(x, tabs, w, tm, mid_dtype):
    rows = x.shape[0]
    period = tabs[0].shape[0] // tm
    row = lambda i: (i, 0)
    tab_spec = pl.BlockSpec((tm, LANES), lambda i: (i % period, 0))
    widths = (RET_QK_W, RET_QK_W, RET_V_W, RET_V_W, D_MODEL, D_MODEL, MLA_PAD_W, MLA_PAD_W)
    out_shape = [jax.ShapeDtypeStruct((rows, n), mid_dtype) for n in widths]
    out_shape += [jax.ShapeDtypeStruct((rows // tm, MLA_V_W, tm), mid_dtype),
                  jax.ShapeDtypeStruct((rows, KV_LORA), F32), jax.ShapeDtypeStruct((rows, ROPE_DIM), F32)]
    out_specs = [pl.BlockSpec((tm, n), row) for n in widths]
    out_specs += [pl.BlockSpec((None, MLA_V_W, tm), lambda i: (i, 0, 0)),
                  pl.BlockSpec((tm, KV_LORA), row), pl.BlockSpec((tm, ROPE_DIM), row)]
    consts = (w['norm_mix'], w['w_in'], w['g_qa'], w['w_qb'], w['g_kva'], w['w_ukp'], w['w_uv_t'],
              w['gq'], w['gk'])
    return pl.pallas_call(
        _proj_kernel,
        grid=(rows // tm,),
        in_specs=[pl.BlockSpec((tm, D_MODEL), row)] + [tab_spec] * 5 + [_const_spec(a.shape) for a in consts],
        out_specs=out_specs,
        out_shape=out_shape,
        compiler_params=_params(("parallel",)),
        name="proj",
    )(x, *tabs, *consts)


def _retention_kernel(q_ref, k_ref, v_ref, rg_ref, dec_ref, qdec_ref, kdec_ref, sdec_ref, s0_ref,
                      o_ref, st_ref, s_scr):
    ci = pl.program_id(1)

    @pl.when(ci == 0)
    def _():
        s_scr[...] = s0_ref[0]

    for h in range(RET_HEADS):
        qh = q_ref[:, h * RET_DK:(h + 1) * RET_DK]
        kh = k_ref[:, h * RET_DK:(h + 1) * RET_DK]
        vh = v_ref[:, h * RET_DV:(h + 1) * RET_DV]
        scores = _dot_nt(qh, kh) * dec_ref[h]
        o = _dot(scores.astype(BF16), vh)
        s_prev = s_scr[h]
        o = o + _dot(qh, s_prev.astype(BF16)) * qdec_ref[h]
        kd = (kh.astype(F32) * kdec_ref[h]).astype(BF16)
        s_scr[h] = sdec_ref[h] * s_prev + lax.dot_general(kd, vh, _TN, preferred_element_type=F32)
        on = o * lax.rsqrt(jnp.mean(o * o, axis=-1, keepdims=True) + EPS)
        g = rg_ref[:, h * RET_DV:(h + 1) * RET_DV].astype(F32)
        o_ref[:, h * RET_DV:(h + 1) * RET_DV] = (g * jax.nn.sigmoid(g) * on).astype(o_ref.dtype)

    @pl.when(ci == pl.num_programs(1) - 1)
    def _():
        st_ref[0] = s_scr[...]


def _retention(rq, rk, rv, rg, s0, dtabs):
    nb, s = rq.shape[:2]
    c = RET_CHUNK
    blk = lambda n: pl.BlockSpec((None, c, n), lambda b, i: (b, i, 0))
    st_shape = (1, RET_HEADS, RET_DK, RET_DV)
    return pl.pallas_call(
        _retention_kernel,
        grid=(nb, s // c),
        in_specs=[blk(RET_QK_W), blk(RET_QK_W), blk(RET_V_W), blk(RET_V_W)]
        + [_const_spec(t.shape) for t in dtabs] + [_const_spec(st_shape)],
        out_specs=[blk(RET_V_W), pl.BlockSpec(st_shape, lambda b, i: (b, 0, 0, 0))],
        out_shape=[jax.ShapeDtypeStruct((nb, s, RET_V_W), BF16),
                   jax.ShapeDtypeStruct((nb,) + st_shape[1:], F32)],
        scratch_shapes=[pltpu.VMEM(st_shape[1:], F32)],
        compiler_params=_params(("parallel", "arbitrary")),
        name="retention",
    )(rq, rk, rv, rg, *dtabs, s0)


ATT_TQ = 512
ATT_TK = 256


def _attention_kernel(q_ref, k_ref, vt_ref, km_ref, vmt_ref, o_ref, qt_scr, s_scr, m_scr, l_scr, acc_scr):
    qi = pl.program_id(1)
    meta_rows = km_ref.shape[0]
    hs = lambda h: slice(h * HEAD_PAD, (h + 1) * HEAD_PAD)
    vs = lambda h: slice(h * V_DIM, (h + 1) * V_DIM)

    qt_scr[...] = q_ref[...].T
    m_scr[...] = jnp.full(m_scr.shape, -jnp.inf, F32)
    l_scr[...] = jnp.zeros(l_scr.shape, F32)
    acc_scr[...] = jnp.zeros(acc_scr.shape, F32)

    def update(n, keys, values_t, mask):
        for h in range(MLA_HEADS):
            s_scr[h, 0:n, :] = _dot(keys(h), qt_scr[hs(h), :])
        for h in range(MLA_HEADS):
            s = s_scr[h, 0:n, :]
            if mask is not None:
                s = jnp.where(mask, s, -jnp.inf)
            m_old = m_scr[h]
            m_new = jnp.maximum(m_old, jnp.max(s, axis=0, keepdims=True))
            alpha = jnp.exp(m_old - m_new)
            p = jnp.exp(s - m_new)
            m_scr[h] = m_new
            l_scr[h] = alpha * l_scr[h] + jnp.sum(p, axis=0, keepdims=True)
            acc_scr[vs(h), :] = alpha * acc_scr[vs(h), :] + _dot(values_t(h), p.astype(BF16))

    meta_ok = lax.broadcasted_iota(jnp.int32, (meta_rows, ATT_TQ), 0) >= meta_rows - N_META
    update(meta_rows, lambda h: km_ref[:, hs(h)], lambda h: vmt_ref[vs(h), :], meta_ok)

    def key_tile(j, mask):
        start = pl.multiple_of(j * ATT_TK, ATT_TK)
        update(ATT_TK, lambda h: k_ref[pl.ds(start, ATT_TK), hs(h)], lambda h: vt_ref[j, vs(h), :], mask)

    def body(j, carry):
        key_tile(j, None)
        return carry

    first_diag = qi * (ATT_TQ // ATT_TK)
    lax.fori_loop(0, first_diag, body, 0)
    key_pos = lax.broadcasted_iota(jnp.int32, (ATT_TK, ATT_TQ), 0)
    query_pos = lax.broadcasted_iota(jnp.int32, (ATT_TK, ATT_TQ), 1)
    for d in range(ATT_TQ // ATT_TK):
        key_tile(first_diag + d, key_pos + d * ATT_TK <= query_pos)

    for h in range(MLA_HEADS):
        acc_scr[vs(h), :] = acc_scr[vs(h), :] / l_scr[h]
    o_ref[...] = acc_scr[...].T.astype(o_ref.dtype)


def _attention(q, k, vt, km, vmt):
    nb, s = q.shape[:2]
    return pl.pallas_call(
        _attention_kernel,
        grid=(nb, s // ATT_TQ),
        in_specs=[pl.BlockSpec((None, ATT_TQ, MLA_PAD_W), lambda b, i: (b, i, 0)),
                  pl.BlockSpec((None, s, MLA_PAD_W), lambda b, i: (b, 0, 0)),
                  pl.BlockSpec((None, s // ATT_TK, MLA_V_W, ATT_TK), lambda b, i: (b, 0, 0, 0)),
                  _const_spec(km.shape), _const_spec(vmt.shape)],
        out_specs=pl.BlockSpec((None, ATT_TQ, MLA_V_W), lambda b, i: (b, i, 0)),
        out_shape=jax.ShapeDtypeStruct((nb, s, MLA_V_W), BF16),
        scratch_shapes=[pltpu.VMEM((MLA_PAD_W, ATT_TQ), BF16),
                        pltpu.VMEM((MLA_HEADS, ATT_TK, ATT_TQ), F32),
                        pltpu.VMEM((MLA_HEADS, 1, ATT_TQ), F32), pltpu.VMEM((MLA_HEADS, 1, ATT_TQ), F32),
                        pltpu.VMEM((MLA_V_W, ATT_TQ), F32)],
        compiler_params=_params(("parallel", "arbitrary")),
        name="attention",
    )(q, k, vt, km, vmt)


FF_BLOCK = 1024


def _finish_kernel(x_ref, ret_ref, mla_ref, gr_ref, gm_ref, wro_ref, wmo_ref, wout_ref, nffn_ref,
                   wup_ref, wdn_ref, y_ref):
    r = _dot(ret_ref[...].astype(BF16), wro_ref[...])
    m = _dot(mla_ref[...].astype(BF16), wmo_ref[...])
    mixed = jax.nn.sigmoid(gr_ref[...].astype(F32)) * r + jax.nn.sigmoid(gm_ref[...].astype(F32)) * m
    h = x_ref[...] + _dot(mixed.astype(BF16), wout_ref[...])
    hn = h * lax.rsqrt(jnp.mean(h * h, axis=-1, keepdims=True) + EPS) * nffn_ref[...]
    hb = hn.astype(BF16)
    y = h
    for f in range(D_FF // FF_BLOCK):
        u = jnp.maximum(_dot(hb, wup_ref[:, f * FF_BLOCK:(f + 1) * FF_BLOCK]), 0.0)
        y = y + _dot((u * u).astype(BF16), wdn_ref[f * FF_BLOCK:(f + 1) * FF_BLOCK, :])
    y_ref[...] = y


def _finish(x, ret_mid, o_mla, gr, gm, w, tm):
    rows = x.shape[0]
    row = lambda i: (i, 0)
    consts = (w['w_ret_o'], w['w_mla_o'], w['w_out'], w['norm_ffn'], w['w_up'], w['w_down'])
    return pl.pallas_call(
        _finish_kernel,
        grid=(rows // tm,),
        in_specs=[pl.BlockSpec((tm, D_MODEL), row), pl.BlockSpec((tm, RET_V_W), row),
                  pl.BlockSpec((tm, MLA_V_W), row), pl.BlockSpec((tm, D_MODEL), row),
                  pl.BlockSpec((tm, D_MODEL), row)] + [_const_spec(a.shape) for a in consts],
        out_specs=pl.BlockSpec((tm, D_MODEL), row),
        out_shape=jax.ShapeDtypeStruct((rows, D_MODEL), F32),
        compiler_params=_params(("parallel",)),
        name="finish",
    )(x, ret_mid, o_mla, gr, gm, *consts)


RET_DEC_GROUP = 8


def _column(row_vec, eye):
    return jnp.sum(jnp.where(eye, row_vec, 0.0), axis=-1, keepdims=True)


def _ret_decode_kernel(q_ref, k_ref, v_ref, rg_ref, gam_ref, s_ref, o_ref, sn_ref):
    eye = (lax.broadcasted_iota(jnp.int32, (RET_DK, RET_DK), 0)
           == lax.broadcasted_iota(jnp.int32, (RET_DK, RET_DK), 1))
    for b in range(RET_DEC_GROUP):
        for h in range(RET_HEADS):
            k_col = _column(k_ref[b:b + 1, h * RET_DK:(h + 1) * RET_DK].astype(F32), eye)
            q_col = _column(q_ref[b:b + 1, h * RET_DK:(h + 1) * RET_DK].astype(F32), eye)
            v_row = v_ref[b:b + 1, h * RET_DV:(h + 1) * RET_DV].astype(F32)
            gam = gam_ref[h]
            s_prev = s_ref[b, h]
            kv = k_col * v_row
            sn_ref[b, h] = gam * s_prev + kv
            o = jnp.sum(q_col * kv, axis=0, keepdims=True) + gam * jnp.sum(q_col * s_prev, axis=0, keepdims=True)
            on = o * lax.rsqrt(jnp.mean(o * o, axis=-1, keepdims=True) + EPS)
            g = rg_ref[b:b + 1, h * RET_DV:(h + 1) * RET_DV].astype(F32)
            o_ref[b:b + 1, h * RET_DV:(h + 1) * RET_DV] = (g * jax.nn.sigmoid(g) * on).astype(o_ref.dtype)


def _ret_decode(rq, rk, rv, rg, gam, state):
    nb = rq.shape[0]
    g = RET_DEC_GROUP
    row = lambda i: (i, 0)
    st_spec = pl.BlockSpec((g, RET_HEADS, RET_DK, RET_DV), lambda i: (i, 0, 0, 0))
    return pl.pallas_call(
        _ret_decode_kernel,
        grid=(nb // g,),
        in_specs=[pl.BlockSpec((g, RET_QK_W), row), pl.BlockSpec((g, RET_QK_W), row),
                  pl.BlockSpec((g, RET_V_W), row), pl.BlockSpec((g, RET_V_W), row),
                  _const_spec(gam.shape), st_spec],
        out_specs=[pl.BlockSpec((g, RET_V_W), row), st_spec],
        out_shape=[jax.ShapeDtypeStruct((nb, RET_V_W), F32), jax.ShapeDtypeStruct(state.shape, F32)],
        compiler_params=_params(("parallel",)),
        name="ret_decode",
    )(rq, rk, rv, rg, gam, state)


def _absorb_kernel(q_ref, k_ref, gk_ref, wukp_ref, qt_ref, qg_ref, s0_ref):
    q = q_ref[...]
    lane = lax.broadcasted_iota(jnp.int32, (q.shape[0], LANES), 1)
    s0 = jnp.zeros((q.shape[0], LANES), F32)
    gk = gk_ref[...]
    for h in range(MLA_HEADS):
        sl = slice(h * HEAD_PAD, (h + 1) * HEAD_PAD)
        qh = q[:, sl]
        s0 = s0 + jnp.where(lane == h, jnp.sum(qh * k_ref[:, sl], axis=-1, keepdims=True), 0.0)
        qg = qh * gk
        qg_ref[:, sl] = qg
        qt_ref[h] = lax.dot_general(qg, wukp_ref[:, sl], _NT, precision=lax.Precision.HIGHEST,
                                    preferred_element_type=F32)
    s0_ref[...] = s0


def _absorb(q, k, gk, wukp):
    nb = q.shape[0]
    return pl.pallas_call(
        _absorb_kernel,
        out_shape=[jax.ShapeDtypeStruct((MLA_HEADS, nb, KV_LORA), F32),
                   jax.ShapeDtypeStruct((nb, MLA_PAD_W), F32),
                   jax.ShapeDtypeStruct((nb, LANES), F32)],
        compiler_params=pltpu.CompilerParams(vmem_limit_bytes=VMEM_LIMIT),
        name="absorb",
    )(q, k, gk, wukp)


DEC_PAGES = 16
DEC_T = DEC_PAGES * PAGE_SIZE
DEC_SUB = 512
UK_ROWS = MLA_HEADS * NOPE_DIM
QT_ROWS = 16


def _mla_decode_kernel(pt_ref, lhsw_ref, qt_ref, qrope_ref, s0_ref, cnew_ref, wuv_ref, ckv_hbm, krt_hbm,
                       o_ref, lhs_scr, cbuf, kbuf, cb_scr, sem):
    b = pl.program_id(0)
    nb = pl.num_programs(0)
    n_groups = pt_ref.shape[1] // DEC_PAGES

    def page_copies(seq, g, slot):
        copies = []
        for i in range(DEC_PAGES):
            page = pt_ref[seq, g * DEC_PAGES + i]
            tok = pl.ds(i * PAGE_SIZE, PAGE_SIZE)
            copies.append(pltpu.make_async_copy(ckv_hbm.at[0, page], cbuf.at[slot, tok], sem.at[0, slot]))
            copies.append(pltpu.make_async_copy(krt_hbm.at[0, page], kbuf.at[slot, :, tok], sem.at[1, slot]))
        return copies

    def start_group(seq, g, slot):
        for cp in page_copies(seq, g, slot):
            cp.start()

    def wait_group(slot):
        for cp in page_copies(0, 0, slot):
            cp.wait()

    def scores(slot):
        pieces = []
        for t in range(DEC_T // DEC_SUB):
            tok = pl.ds(t * DEC_SUB, DEC_SUB)
            cb = cbuf[slot, tok, :].astype(BF16)
            cb_scr[slot, tok, :] = cb
            big = _dot_nt(lhs_scr[...], cb)
            sq = [big[d * MLA_HEADS:(d + 1) * MLA_HEADS] for d in range(NOPE_DIM)]
            sq = [x * x for x in sq]
            while len(sq) > 1:
                sq = [sq[i] + sq[i + 1] for i in range(0, len(sq), 2)]
            krt = kbuf[slot, :, tok]
            kk = krt * krt
            kk = (kk[0:8] + kk[8:16]) + (kk[16:24] + kk[24:32])
            ss = sq[0] + jnp.sum(kk, axis=0, keepdims=True)
            num = big[UK_ROWS:UK_ROWS + MLA_HEADS] + _dot(qrope_ref[0], krt.astype(BF16))[0:MLA_HEADS]
            pieces.append(num * lax.rsqrt(ss * (1.0 / QK_DIM) + EPS))
        return jnp.concatenate(pieces, axis=-1)

    def softmax_update(s, slot, m, l, acc):
        m_new = jnp.maximum(m, jnp.max(s, axis=-1, keepdims=True))
        alpha = jnp.exp(m - m_new)
        p = jnp.exp(s - m_new)
        l = alpha * l + jnp.sum(p, axis=-1, keepdims=True)
        acc = alpha * acc + _dot(p.astype(BF16), cb_scr[slot])
        return m_new, l, acc

    @pl.when(b == 0)
    def _():
        start_group(0, 0, 0)

    lhs_scr[0:UK_ROWS, :] = lhsw_ref[...]
    lhs_scr[UK_ROWS:UK_ROWS + QT_ROWS, :] = qt_ref[0]

    wait_group(0)
    start_group(b, 1, 1)
    s_first = scores(0)

    def body(g, carry):
        s_prev, m, l, acc = carry
        slot = g % 2
        wait_group(slot)

        wrap = g + 1 == n_groups

        @pl.when(jnp.logical_or(jnp.logical_not(wrap), b + 1 < nb))
        def _():
            start_group(jnp.where(wrap, b + 1, b), jnp.where(wrap, 0, g + 1), 1 - slot)

        s_cur = scores(slot)
        m, l, acc = softmax_update(s_prev, 1 - slot, m, l, acc)
        return s_cur, m, l, acc

    init = (s_first, s0_ref[0], jnp.ones((MLA_HEADS, 1), F32),
            jnp.broadcast_to(cnew_ref[0], (MLA_HEADS, KV_LORA)))
    s_last, m, l, acc = lax.fori_loop(1, n_groups, body, init)
    m, l, acc = softmax_update(s_last, (n_groups - 1) % 2, m, l, acc)
    lat = acc / l
    full = _dot(lat.astype(BF16), wuv_ref[...])
    own = (lax.broadcasted_iota(jnp.int32, full.shape, 1) // V_DIM
           == lax.broadcasted_iota(jnp.int32, full.shape, 0))
    o_ref[0] = jnp.sum(jnp.where(own, full, 0.0), axis=0, keepdims=True)


def _mla_decode(page_table, lhsw, qt, qrope, s0, c_new, wuv, cache_ckv, cache_krt):
    nb, n_pages = page_table.shape
    assert n_pages % (2 * DEC_PAGES) == 0
    per_seq = lambda shape: pl.BlockSpec((1,) + shape, lambda i, pt: (i, 0, 0))
    const = lambda a: pl.BlockSpec(a.shape, lambda i, pt: (0,) * a.ndim, pipeline_mode=pl.Buffered(1))
    grid_spec = pltpu.PrefetchScalarGridSpec(
        num_scalar_prefetch=1,
        grid=(nb,),
        in_specs=[const(lhsw), per_seq((QT_ROWS, KV_LORA)), per_seq((QT_ROWS, ROPE_DIM)),
                  per_seq((MLA_HEADS, 1)), per_seq((1, KV_LORA)), const(wuv),
                  pl.BlockSpec(memory_space=pl.ANY), pl.BlockSpec(memory_space=pl.ANY)],
        out_specs=per_seq((1, MLA_V_W)),
        scratch_shapes=[pltpu.VMEM((UK_ROWS + QT_ROWS, KV_LORA), BF16),
                        pltpu.VMEM((2, DEC_T, KV_LORA), F32),
                        pltpu.VMEM((2, ROPE_DIM, DEC_T), F32),
                        pltpu.VMEM((2, DEC_T, KV_LORA), BF16),
                        pltpu.SemaphoreType.DMA((2, 2))],
    )
    return pl.pallas_call(
        _mla_decode_kernel,
        grid_spec=grid_spec,
        out_shape=jax.ShapeDtypeStruct((nb, 1, MLA_V_W), F32),
        compiler_params=_params(("arbitrary",)),
        name="mla_decode",
    )(page_table, lhsw, qt, qrope, s0, c_new, wuv, cache_ckv, cache_krt)


def _rotary_tables(pos):
    pos = pos.astype(F32)[:, None]

    def cos_sin(half):
        inv = ROPE_BASE ** (-jnp.arange(half, dtype=F32) / half)
        ang = pos * inv[None, :]
        return jnp.cos(ang), jnp.sin(ang)

    cr, sr = cos_sin(RET_DK // 2)
    cosr = jnp.concatenate([cr, cr], axis=-1)
    sinr = jnp.concatenate([-sr, sr], axis=-1)
    c, s = cos_sin(ROPE_DIM // 2)
    n = pos.shape[0]
    half = ROPE_DIM // 2
    z = lambda w: jnp.zeros((n, w), F32)
    cm = jnp.concatenate([jnp.ones((n, NOPE_DIM), F32), c, c, z(HEAD_PAD - QK_DIM)], axis=-1)
    sa = jnp.concatenate([z(NOPE_DIM), -s, z(half), z(HEAD_PAD - QK_DIM)], axis=-1)
    sb = jnp.concatenate([z(NOPE_DIM), z(half), s, z(HEAD_PAD - QK_DIM)], axis=-1)
    return cosr, sinr, cm, sa, sb


def _decay_tables(c):
    lg = jnp.log1p(-jnp.exp2(-5.0 - jnp.arange(RET_HEADS, dtype=F32)))
    idx = jnp.arange(c, dtype=F32)
    diff = idx[:, None] - idx[None, :]
    dec = jnp.where(diff[None] >= 0, jnp.exp(jnp.maximum(diff, 0.0)[None] * lg[:, None, None]), 0.0)
    qdec = jnp.exp((idx + 1.0)[None, :] * lg[:, None])[:, :, None]
    kdec = jnp.exp((c - 1.0 - idx)[None, :] * lg[:, None])[:, :, None]
    sdec = jnp.exp(c * lg)[:, None, None]
    return dec, qdec, kdec, sdec


def _pad_heads(a):
    a = jnp.pad(a, [(0, 0)] * (a.ndim - 1) + [(0, HEAD_PAD - a.shape[-1])])
    return a.reshape(a.shape[:-2] + (a.shape[-2] * HEAD_PAD,))


def _layer_weights(norm_mix, w_in, g_qa, w_qb, g_kva, w_uk, w_uv, g_qn, g_kn, w_ret_o, w_mla_o, w_out,
                   norm_ffn, w_up, w_down):
    idx = [int(i) for i in np.cumsum(SPLITS)[:-1]]
    parts = jnp.split(w_in, idx, axis=-1)
    parts[6] = jnp.pad(parts[6], ((0, 0), (NOPE_DIM, HEAD_PAD - QK_DIM)))
    pad1 = lambda g: jnp.pad(g, (0, HEAD_PAD - QK_DIM))[None, :]
    wukp = _pad_heads(w_uk)
    return {
        'norm_mix': norm_mix[None, :],
        'w_in': jnp.concatenate(parts, axis=-1).astype(BF16),
        'g_qa': g_qa[None, :],
        'w_qb': _pad_heads(w_qb.reshape(Q_LORA, MLA_HEADS, QK_DIM)).astype(BF16),
        'g_kva': g_kva[None, :],
        'w_ukp': wukp.astype(BF16),
        'w_ukp_f32': wukp,
        'w_uk_t': w_uk.transpose(2, 1, 0).reshape(UK_ROWS, KV_LORA).astype(BF16),
        'w_uv': w_uv.reshape(KV_LORA, MLA_V_W).astype(BF16),
        'w_uv_t': w_uv.reshape(KV_LORA, MLA_V_W).T.astype(BF16),
        'gq': pad1(g_qn) * (QK_DIM ** -0.5),
        'gk': pad1(g_kn),
        'w_ret_o': w_ret_o.astype(BF16),
        'w_mla_o': w_mla_o.astype(BF16),
        'w_out': w_out.astype(BF16),
        'norm_ffn': norm_ffn[None, :],
        'w_up': w_up.astype(BF16),
        'w_down': w_down.astype(BF16),
    }


FINISH_TM = 512


def kernel(x_prompt, x_sample, cache_ckv, cache_krope, state_ret, page_table, meta_tokens, norm_mix, w_in,
           g_qa, w_qb, g_kva, w_uk, w_uv, g_qn, g_kn, w_ret_o, w_mla_o, w_out, norm_ffn, w_up, w_down):
    nb, seq, _ = x_prompt.shape
    ns = x_sample.shape[0]
    assert x_sample.shape[1] == 1 and norm_mix.shape[0] == 1
    past = page_table.shape[1] * PAGE_SIZE
    w = _layer_weights(norm_mix[0], w_in[0], g_qa[0], w_qb[0], g_kva[0], w_uk[0], w_uv[0], g_qn[0], g_kn[0],
                       w_ret_o[0], w_mla_o[0], w_out[0], norm_ffn[0], w_up[0], w_down[0])
    dtabs = _decay_tables(RET_CHUNK)

    mpad = RET_CHUNK - N_META
    xm = jnp.pad(meta_tokens, ((mpad, 0), (0, 0)))
    pos_m = jnp.maximum(jnp.arange(RET_CHUNK) - mpad, 0)
    rq, rk, rv, rg, _, _, _, km, vmt, c_m, kr_m = _proj(xm, _rotary_tables(pos_m), w, RET_CHUNK, BF16)
    zero_state = jnp.zeros((1, RET_HEADS, RET_DK, RET_DV), F32)
    _, s0 = _retention(rq[None], rk[None], rv[None], rg[None], zero_state, dtabs)

    xp = x_prompt.reshape(nb * seq, D_MODEL)
    rq, rk, rv, rg, gr, gm, q, k, vt, c_p, kr_p = _proj(
        xp, _rotary_tables(N_META + jnp.arange(seq)), w, ATT_TK, BF16)
    b3 = lambda a: a.reshape(nb, seq, a.shape[-1])
    ret_mid, st_p = _retention(b3(rq), b3(rk), b3(rv), b3(rg), s0, dtabs)
    o_mla = _attention(b3(q), b3(k), vt.reshape(nb, seq // ATT_TK, MLA_V_W, ATT_TK), km, vmt[0])
    y_prompt = _finish(xp, ret_mid.reshape(nb * seq, RET_V_W), o_mla.reshape(nb * seq, MLA_V_W), gr, gm, w,
                       FINISH_TM).reshape(nb, seq, D_MODEL)
    bcast = lambda a: jnp.broadcast_to(a[None, mpad:], (nb, N_META, a.shape[-1]))
    ckv_prompt = jnp.concatenate([bcast(c_m), b3(c_p)], axis=1)[None]
    krope_prompt = jnp.concatenate([bcast(kr_m), b3(kr_p)], axis=1)[None]

    xs = x_sample.reshape(ns, D_MODEL)
    pos_s = jnp.full((ns,), past, jnp.int32)
    rq, rk, rv, rg, gr, gm, q, k, _, c_s, kr_s = _proj(xs, _rotary_tables(pos_s), w, ns, F32)
    gam = jnp.exp(jnp.log1p(-jnp.exp2(-5.0 - jnp.arange(RET_HEADS, dtype=F32))))[:, None, None]
    ret_mid, st_s = _ret_decode(rq, rk, rv, rg, gam, state_ret[0])
    qt, qg, s0_new = _absorb(q, k, w['gk'], w['w_ukp_f32'])
    qt = jnp.pad(qt.transpose(1, 0, 2), ((0, 0), (0, QT_ROWS - MLA_HEADS), (0, 0))).astype(BF16)
    q_rope = qg.reshape(ns, MLA_HEADS, HEAD_PAD)[:, :, NOPE_DIM:QK_DIM]
    q_rope = jnp.pad(q_rope, ((0, 0), (0, QT_ROWS - MLA_HEADS), (0, 0))).astype(BF16)
    o_mla = _mla_decode(page_table, w['w_uk_t'], qt, q_rope, s0_new[:, :MLA_HEADS, None], c_s[:, None, :],
                        w['w_uv'], cache_ckv, jnp.swapaxes(cache_krope, 2, 3))
    y_sample = _finish(xs, ret_mid, o_mla.reshape(ns, MLA_V_W), gr, gm, w, ns).reshape(ns, 1, D_MODEL)

    return (y_prompt, y_sample, ckv_prompt, krope_prompt, st_p[None],
            c_s.reshape(1, ns, 1, KV_LORA), kr_s.reshape(1, ns, 1, ROPE_DIM), st_s[None])
```

```python
import numpy as np
import jax
import jax.numpy as jnp
from jax import lax
from jax.experimental import pallas as pl
from jax.experimental.pallas import tpu as pltpu

D_MODEL = 1024
N_META = 16
RET_HEADS = 4
RET_DK = 128
RET_DV = 256
RET_CHUNK = 128
MLA_HEADS = 8
Q_LORA = 384
KV_LORA = 256
NOPE_DIM = 64
ROPE_DIM = 32
QK_DIM = NOPE_DIM + ROPE_DIM
V_DIM = 64
D_FF = 4 * D_MODEL
PAGE_SIZE = 128
ROPE_BASE = 10000.0
EPS = 1e-6

RET_QK_W = RET_HEADS * RET_DK
RET_V_W = RET_HEADS * RET_DV
MLA_V_W = MLA_HEADS * V_DIM
SPLITS = (RET_QK_W, RET_QK_W, RET_V_W, RET_V_W, Q_LORA, KV_LORA, ROPE_DIM, D_MODEL, D_MODEL)

LANES = 128
HEAD_PAD = LANES
MLA_PAD_W = MLA_HEADS * HEAD_PAD
_SEG_W = (RET_QK_W, RET_QK_W, RET_V_W, RET_V_W, Q_LORA, KV_LORA, LANES, D_MODEL, D_MODEL)
_SEG_O = tuple(int(v) for v in np.cumsum((0,) + _SEG_W))
IN_PAD_W = _SEG_O[-1]
VMEM_LIMIT = 56 * 1024 * 1024

F32 = jnp.float32
BF16 = jnp.bfloat16
_NT = (((1,), (1,)), ((), ()))
_TN = (((0,), (0,)), ((), ()))


def _dot(a, b):
    return jnp.dot(a, b, preferred_element_type=F32)


def _dot_nt(a, b):
    return lax.dot_general(a, b, _NT, preferred_element_type=F32)


def _const_spec(shape):
    zeros = (0,) * len(shape)
    return pl.BlockSpec(shape, lambda *_: zeros, pipeline_mode=pl.Buffered(1))


def _params(sem):
    return pltpu.CompilerParams(dimension_semantics=sem, vmem_limit_bytes=VMEM_LIMIT)


def _proj_kernel(x_ref, cosr_ref, sinr_ref, cm_ref, sa_ref, sb_ref, nmix_ref, win_ref, gqa_ref,
                 wqb_ref, gkva_ref, wukp_ref, wuvt_ref, gq_ref, gk_ref,
                 rq_ref, rk_ref, rv_ref, rg_ref, gr_ref, gm_ref, q_ref, k_ref, vt_ref, c_ref, kr_ref):
    x = x_ref[...]
    xn = x * lax.rsqrt(jnp.mean(x * x, axis=-1, keepdims=True) + EPS) * nmix_ref[...]
    xb = xn.astype(BF16)

    def seg(i):
        return _dot(xb, win_ref[:, _SEG_O[i]:_SEG_O[i + 1]])

    cosr, sinr = cosr_ref[...], sinr_ref[...]
    cm, sa, sb = cm_ref[...], sa_ref[...], sb_ref[...]

    def ret_rotary(z, out_ref, scale):
        for h in range(RET_HEADS):
            zh = z[:, h * RET_DK:(h + 1) * RET_DK]
            r = zh * cosr + pltpu.roll(zh, RET_DK // 2, 1) * sinr
            if scale is not None:
                r = r * scale
            out_ref[:, h * RET_DK:(h + 1) * RET_DK] = r.astype(out_ref.dtype)

    def mla_rotary(zh):
        half = ROPE_DIM // 2
        return zh * cm + pltpu.roll(zh, HEAD_PAD - half, 1) * sa + pltpu.roll(zh, half, 1) * sb

    def head_norm(zh, g):
        ss = jnp.sum(zh * zh, axis=-1, keepdims=True)
        return zh * lax.rsqrt(ss * (1.0 / QK_DIM) + EPS) * g

    qa = seg(4)
    qn = qa * lax.rsqrt(jnp.mean(qa * qa, axis=-1, keepdims=True) + EPS) * gqa_ref[...]
    ckv = seg(5)
    c = ckv * lax.rsqrt(jnp.mean(ckv * ckv, axis=-1, keepdims=True) + EPS) * gkva_ref[...]
    c_ref[...] = c
    cb = c.astype(BF16)
    krp = mla_rotary(seg(6))
    kr_ref[...] = krp[:, NOPE_DIM:QK_DIM]

    ret_rotary(seg(0), rq_ref, None)
    ret_rotary(seg(1), rk_ref, RET_DK ** -0.5)
    rv_ref[...] = seg(2).astype(rv_ref.dtype)
    rg_ref[...] = seg(3).astype(rg_ref.dtype)
    gr_ref[...] = seg(7).astype(gr_ref.dtype)
    gm_ref[...] = seg(8).astype(gm_ref.dtype)

    qp = _dot(qn.astype(BF16), wqb_ref[...])
    gq = gq_ref[...]
    for h in range(MLA_HEADS):
        qh = mla_rotary(qp[:, h * HEAD_PAD:(h + 1) * HEAD_PAD])
        q_ref[:, h * HEAD_PAD:(h + 1) * HEAD_PAD] = head_norm(qh, gq).astype(q_ref.dtype)

    kp = _dot(cb, wukp_ref[...])
    gk = gk_ref[...]
    for h in range(MLA_HEADS):
        kh = kp[:, h * HEAD_PAD:(h + 1) * HEAD_PAD] + krp
        k_ref[:, h * HEAD_PAD:(h + 1) * HEAD_PAD] = head_norm(kh, gk).astype(k_ref.dtype)
    vt_w = vt_ref.shape[-1]
    for t in range(vt_ref.shape[0]):
        vt_ref[t] = _dot_nt(wuvt_ref[...], cb[t * vt_w:(t + 1) * vt_w]).astype(vt_ref.dtype)


def _proj(x, tabs, w, tm, vt_w, mid_dtype):
    rows = x.shape[0]
    period = tabs[0].shape[0] // tm
    row = lambda i: (i, 0)
    tab_spec = pl.BlockSpec((tm, LANES), lambda i: (i % period, 0))
    widths = (RET_QK_W, RET_QK_W, RET_V_W, RET_V_W, D_MODEL, D_MODEL, MLA_PAD_W, MLA_PAD_W)
    out_shape = [jax.ShapeDtypeStruct((rows, n), mid_dtype) for n in widths]
    out_shape += [jax.ShapeDtypeStruct((rows // vt_w, MLA_V_W, vt_w), mid_dtype),
                  jax.ShapeDtypeStruct((rows, KV_LORA), F32), jax.ShapeDtypeStruct((rows, ROPE_DIM), F32)]
    out_specs = [pl.BlockSpec((tm, n), row) for n in widths]
    out_specs += [pl.BlockSpec((tm // vt_w, MLA_V_W, vt_w), lambda i: (i, 0, 0)),
                  pl.BlockSpec((tm, KV_LORA), row), pl.BlockSpec((tm, ROPE_DIM), row)]
    consts = (w['norm_mix'], w['w_in'], w['g_qa'], w['w_qb'], w['g_kva'], w['w_ukp'], w['w_uv_t'],
              w['gq'], w['gk'])
    return pl.pallas_call(
        _proj_kernel,
        grid=(rows // tm,),
        in_specs=[pl.BlockSpec((tm, D_MODEL), row)] + [tab_spec] * 5 + [_const_spec(a.shape) for a in consts],
        out_specs=out_specs,
        out_shape=out_shape,
        compiler_params=_params(("parallel",)),
        name="proj",
    )(x, *tabs, *consts)


def _retention_kernel(q_ref, k_ref, v_ref, rg_ref, dec_ref, qdec_ref, kdec_ref, sdec_ref, s0_ref,
                      o_ref, st_ref, s_scr):
    ci = pl.program_id(1)

    @pl.when(ci == 0)
    def _():
        for r in range(s_scr.shape[0]):
            s_scr[r] = s0_ref[0]

    for r in range(s_scr.shape[0]):
        for h in range(RET_HEADS):
            qh = q_ref[r, :, h * RET_DK:(h + 1) * RET_DK]
            kh = k_ref[r, :, h * RET_DK:(h + 1) * RET_DK]
            vh = v_ref[r, :, h * RET_DV:(h + 1) * RET_DV]
            scores = _dot_nt(qh, kh) * dec_ref[h]
            o = _dot(scores.astype(BF16), vh)
            s_prev = s_scr[r, h]
            o = o + _dot(qh, s_prev.astype(BF16)) * qdec_ref[h]
            kd = (kh.astype(F32) * kdec_ref[h]).astype(BF16)
            s_scr[r, h] = sdec_ref[h] * s_prev + lax.dot_general(kd, vh, _TN, preferred_element_type=F32)
            on = o * lax.rsqrt(jnp.mean(o * o, axis=-1, keepdims=True) + EPS)
            g = rg_ref[r, :, h * RET_DV:(h + 1) * RET_DV].astype(F32)
            o_ref[r, :, h * RET_DV:(h + 1) * RET_DV] = (g * jax.nn.sigmoid(g) * on).astype(o_ref.dtype)

    @pl.when(ci == pl.num_programs(1) - 1)
    def _():
        st_ref[...] = s_scr[...]


RET_BATCH = 4


def _retention(rq, rk, rv, rg, s0, dtabs):
    nb, s = rq.shape[:2]
    c = RET_CHUNK
    rb = min(nb, RET_BATCH)
    blk = lambda n: pl.BlockSpec((rb, c, n), lambda b, i: (b, i, 0))
    st_shape = (RET_HEADS, RET_DK, RET_DV)
    return pl.pallas_call(
        _retention_kernel,
        grid=(nb // rb, s // c),
        in_specs=[blk(RET_QK_W), blk(RET_QK_W), blk(RET_V_W), blk(RET_V_W)]
        + [_const_spec(t.shape) for t in dtabs] + [_const_spec((1,) + st_shape)],
        out_specs=[blk(RET_V_W), pl.BlockSpec((rb,) + st_shape, lambda b, i: (b, 0, 0, 0))],
        out_shape=[jax.ShapeDtypeStruct((nb, s, RET_V_W), BF16),
                   jax.ShapeDtypeStruct((nb,) + st_shape, F32)],
        scratch_shapes=[pltpu.VMEM((rb,) + st_shape, F32)],
        compiler_params=_params(("parallel", "arbitrary")),
        name="retention",
    )(rq, rk, rv, rg, *dtabs, s0)


ATT_TQ = 512
ATT_TK = 256


def _attention_kernel(q_ref, k_ref, vt_ref, km_ref, vmt_ref, o_ref, qt_scr, s_scr, m_scr, l_scr, acc_scr):
    qi = pl.program_id(1)
    meta_rows = km_ref.shape[0]
    hs = lambda h: slice(h * HEAD_PAD, (h + 1) * HEAD_PAD)
    vs = lambda h: slice(h * V_DIM, (h + 1) * V_DIM)

    qt_scr[...] = q_ref[...].T
    m_scr[...] = jnp.full(m_scr.shape, -jnp.inf, F32)
    l_scr[...] = jnp.zeros(l_scr.shape, F32)
    acc_scr[...] = jnp.zeros(acc_scr.shape, F32)

    def update(n, keys, values_t, mask):
        for h in range(MLA_HEADS):
            s_scr[h, 0:n, :] = _dot(keys(h), qt_scr[hs(h), :])
        for h in range(MLA_HEADS):
            s = s_scr[h, 0:n, :]
            if mask is not None:
                s = jnp.where(mask, s, -jnp.inf)
            m_old = m_scr[h]
            m_new = jnp.maximum(m_old, jnp.max(s, axis=0, keepdims=True))
            alpha = jnp.exp(m_old - m_new)
            p = jnp.exp(s - m_new)
            m_scr[h] = m_new
            l_scr[h] = alpha * l_scr[h] + jnp.sum(p, axis=0, keepdims=True)
            acc_scr[vs(h), :] = alpha * acc_scr[vs(h), :] + _dot(values_t(h), p.astype(BF16))

    meta_ok = lax.broadcasted_iota(jnp.int32, (meta_rows, ATT_TQ), 0) >= meta_rows - N_META
    update(meta_rows, lambda h: km_ref[:, hs(h)], lambda h: vmt_ref[vs(h), :], meta_ok)

    def key_tile(j, mask):
        start = pl.multiple_of(j * ATT_TK, ATT_TK)
        update(ATT_TK, lambda h: k_ref[pl.ds(start, ATT_TK), hs(h)], lambda h: vt_ref[j, vs(h), :], mask)

    def body(j, carry):
        key_tile(j, None)
        return carry

    first_diag = qi * (ATT_TQ // ATT_TK)
    lax.fori_loop(0, first_diag, body, 0)
    key_pos = lax.broadcasted_iota(jnp.int32, (ATT_TK, ATT_TQ), 0)
    query_pos = lax.broadcasted_iota(jnp.int32, (ATT_TK, ATT_TQ), 1)
    for d in range(ATT_TQ // ATT_TK):
        key_tile(first_diag + d, key_pos + d * ATT_TK <= query_pos)

    for h in range(MLA_HEADS):
        acc_scr[vs(h), :] = acc_scr[vs(h), :] / l_scr[h]
    o_ref[...] = acc_scr[...].T.astype(o_ref.dtype)


def _attention(q, k, vt, km, vmt):
    nb, s = q.shape[:2]
    return pl.pallas_call(
        _attention_kernel,
        grid=(nb, s // ATT_TQ),
        in_specs=[pl.BlockSpec((None, ATT_TQ, MLA_PAD_W), lambda b, i: (b, i, 0)),
                  pl.BlockSpec((None, s, MLA_PAD_W), lambda b, i: (b, 0, 0)),
                  pl.BlockSpec((None, s // ATT_TK, MLA_V_W, ATT_TK), lambda b, i: (b, 0, 0, 0)),
                  _const_spec(km.shape), _const_spec(vmt.shape)],
        out_specs=pl.BlockSpec((None, ATT_TQ, MLA_V_W), lambda b, i: (b, i, 0)),
        out_shape=jax.ShapeDtypeStruct((nb, s, MLA_V_W), BF16),
        scratch_shapes=[pltpu.VMEM((MLA_PAD_W, ATT_TQ), BF16),
                        pltpu.VMEM((MLA_HEADS, ATT_TK, ATT_TQ), F32),
                        pltpu.VMEM((MLA_HEADS, 1, ATT_TQ), F32), pltpu.VMEM((MLA_HEADS, 1, ATT_TQ), F32),
                        pltpu.VMEM((MLA_V_W, ATT_TQ), F32)],
        compiler_params=_params(("parallel", "arbitrary")),
        name="attention",
    )(q, k, vt, km, vmt)


FF_BLOCK = 1024


def _finish_kernel(x_ref, ret_ref, mla_ref, gr_ref, gm_ref, wro_ref, wmo_ref, wout_ref, nffn_ref,
                   wup_ref, wdn_ref, y_ref):
    r = _dot(ret_ref[...].astype(BF16), wro_ref[...])
    m = _dot(mla_ref[...].astype(BF16), wmo_ref[...])
    mixed = jax.nn.sigmoid(gr_ref[...].astype(F32)) * r + jax.nn.sigmoid(gm_ref[...].astype(F32)) * m
    h = x_ref[...] + _dot(mixed.astype(BF16), wout_ref[...])
    hn = h * lax.rsqrt(jnp.mean(h * h, axis=-1, keepdims=True) + EPS) * nffn_ref[...]
    hb = hn.astype(BF16)
    y = h
    for f in range(D_FF // FF_BLOCK):
        u = jnp.maximum(_dot(hb, wup_ref[:, f * FF_BLOCK:(f + 1) * FF_BLOCK]), 0.0)
        y = y + _dot((u * u).astype(BF16), wdn_ref[f * FF_BLOCK:(f + 1) * FF_BLOCK, :])
    y_ref[...] = y


def _finish(x, ret_mid, o_mla, gr, gm, w, tm):
    rows = x.shape[0]
    row = lambda i: (i, 0)
    consts = (w['w_ret_o'], w['w_mla_o'], w['w_out'], w['norm_ffn'], w['w_up'], w['w_down'])
    return pl.pallas_call(
        _finish_kernel,
        grid=(rows // tm,),
        in_specs=[pl.BlockSpec((tm, D_MODEL), row), pl.BlockSpec((tm, RET_V_W), row),
                  pl.BlockSpec((tm, MLA_V_W), row), pl.BlockSpec((tm, D_MODEL), row),
                  pl.BlockSpec((tm, D_MODEL), row)] + [_const_spec(a.shape) for a in consts],
        out_specs=pl.BlockSpec((tm, D_MODEL), row),
        out_shape=jax.ShapeDtypeStruct((rows, D_MODEL), F32),
        compiler_params=_params(("parallel",)),
        name="finish",
    )(x, ret_mid, o_mla, gr, gm, *consts)


RET_DEC_GROUP = 8


def _column(row_vec, eye):
    return jnp.sum(jnp.where(eye, row_vec, 0.0), axis=-1, keepdims=True)


def _ret_decode_kernel(q_ref, k_ref, v_ref, rg_ref, gam_ref, s_ref, o_ref, sn_ref):
    eye = (lax.broadcasted_iota(jnp.int32, (RET_DK, RET_DK), 0)
           == lax.broadcasted_iota(jnp.int32, (RET_DK, RET_DK), 1))
    for b in range(RET_DEC_GROUP):
        for h in range(RET_HEADS):
            k_col = _column(k_ref[b:b + 1, h * RET_DK:(h + 1) * RET_DK].astype(F32), eye)
            q_col = _column(q_ref[b:b + 1, h * RET_DK:(h + 1) * RET_DK].astype(F32), eye)
            v_row = v_ref[b:b + 1, h * RET_DV:(h + 1) * RET_DV].astype(F32)
            gam = gam_ref[h]
            s_prev = s_ref[b, h]
            kv = k_col * v_row
            sn_ref[b, h] = gam * s_prev + kv
            o = jnp.sum(q_col * kv, axis=0, keepdims=True) + gam * jnp.sum(q_col * s_prev, axis=0, keepdims=True)
            on = o * lax.rsqrt(jnp.mean(o * o, axis=-1, keepdims=True) + EPS)
            g = rg_ref[b:b + 1, h * RET_DV:(h + 1) * RET_DV].astype(F32)
            o_ref[b:b + 1, h * RET_DV:(h + 1) * RET_DV] = (g * jax.nn.sigmoid(g) * on).astype(o_ref.dtype)


def _ret_decode(rq, rk, rv, rg, gam, state):
    nb = rq.shape[0]
    g = RET_DEC_GROUP
    row = lambda i: (i, 0)
    st_spec = pl.BlockSpec((g, RET_HEADS, RET_DK, RET_DV), lambda i: (i, 0, 0, 0))
    return pl.pallas_call(
        _ret_decode_kernel,
        grid=(nb // g,),
        in_specs=[pl.BlockSpec((g, RET_QK_W), row), pl.BlockSpec((g, RET_QK_W), row),
                  pl.BlockSpec((g, RET_V_W), row), pl.BlockSpec((g, RET_V_W), row),
                  _const_spec(gam.shape), st_spec],
        out_specs=[pl.BlockSpec((g, RET_V_W), row), st_spec],
        out_shape=[jax.ShapeDtypeStruct((nb, RET_V_W), F32), jax.ShapeDtypeStruct(state.shape, F32)],
        compiler_params=_params(("parallel",)),
        name="ret_decode",
    )(rq, rk, rv, rg, gam, state)


def _absorb_kernel(q_ref, k_ref, gk_ref, wukp_ref, qt_ref, qg_ref, s0_ref):
    q = q_ref[...]
    lane = lax.broadcasted_iota(jnp.int32, (q.shape[0], LANES), 1)
    s0 = jnp.zeros((q.shape[0], LANES), F32)
    gk = gk_ref[...]
    for h in range(MLA_HEADS):
        sl = slice(h * HEAD_PAD, (h + 1) * HEAD_PAD)
        qh = q[:, sl]
        s0 = s0 + jnp.where(lane == h, jnp.sum(qh * k_ref[:, sl], axis=-1, keepdims=True), 0.0)
        qg = qh * gk
        qg_ref[:, sl] = qg
        qt_ref[h] = lax.dot_general(qg, wukp_ref[:, sl], _NT, precision=lax.Precision.HIGHEST,
                                    preferred_element_type=F32)
    s0_ref[...] = s0


def _absorb(q, k, gk, wukp):
    nb = q.shape[0]
    return pl.pallas_call(
        _absorb_kernel,
        out_shape=[jax.ShapeDtypeStruct((MLA_HEADS, nb, KV_LORA), F32),
                   jax.ShapeDtypeStruct((nb, MLA_PAD_W), F32),
                   jax.ShapeDtypeStruct((nb, LANES), F32)],
        compiler_params=pltpu.CompilerParams(vmem_limit_bytes=VMEM_LIMIT),
        name="absorb",
    )(q, k, gk, wukp)


DEC_PAGES = 16
DEC_T = DEC_PAGES * PAGE_SIZE
DEC_SUB = 512
UK_ROWS = MLA_HEADS * NOPE_DIM
QT_ROWS = 16
DEC_AHEAD = 2
DEC_SLOTS = DEC_AHEAD + 1


def _mla_decode_kernel(pt_ref, lhsw_ref, qt_ref, qrope_ref, s0_ref, cnew_ref, wuv_ref, ckv_hbm, krt_hbm,
                       o_ref, lhs_scr, cbuf, kbuf, cb_scr, sem):
    b = pl.program_id(0)
    nb = pl.num_programs(0)
    n_groups = pt_ref.shape[1] // DEC_PAGES

    def page_copies(seq, g, slot):
        copies = []
        for i in range(DEC_PAGES):
            page = pt_ref[seq, g * DEC_PAGES + i]
            tok = pl.ds(i * PAGE_SIZE, PAGE_SIZE)
            copies.append(pltpu.make_async_copy(ckv_hbm.at[0, page], cbuf.at[slot, tok], sem.at[0, slot]))
            copies.append(pltpu.make_async_copy(krt_hbm.at[0, page], kbuf.at[slot, :, tok], sem.at[1, slot]))
        return copies

    def start_group(seq, g, slot):
        for cp in page_copies(seq, g, slot):
            cp.start()

    def wait_group(slot):
        for cp in page_copies(0, 0, slot):
            cp.wait()

    def scores(slot, cb_slot):
        pieces = []
        for t in range(DEC_T // DEC_SUB):
            tok = pl.ds(t * DEC_SUB, DEC_SUB)
            cb = cbuf[slot, tok, :].astype(BF16)
            cb_scr[cb_slot, tok, :] = cb
            big = _dot_nt(lhs_scr[...], cb)
            sq = [big[d * MLA_HEADS:(d + 1) * MLA_HEADS] for d in range(NOPE_DIM)]
            sq = [x * x for x in sq]
            while len(sq) > 1:
                sq = [sq[i] + sq[i + 1] for i in range(0, len(sq), 2)]
            krt = kbuf[slot, :, tok]
            kk = krt * krt
            kk = (kk[0:8] + kk[8:16]) + (kk[16:24] + kk[24:32])
            ss = sq[0] + jnp.sum(kk, axis=0, keepdims=True)
            num = big[UK_ROWS:UK_ROWS + MLA_HEADS] + _dot(qrope_ref[0], krt.astype(BF16))[0:MLA_HEADS]
            pieces.append(num * lax.rsqrt(ss * (1.0 / QK_DIM) + EPS))
        return jnp.concatenate(pieces, axis=-1)

    def softmax_update(s, slot, m, l, acc):
        m_new = jnp.maximum(m, jnp.max(s, axis=-1, keepdims=True))
        alpha = jnp.exp(m - m_new)
        p = jnp.exp(s - m_new)
        l = alpha * l + jnp.sum(p, axis=-1, keepdims=True)
        acc = alpha * acc + _dot(p.astype(BF16), cb_scr[slot])
        return m_new, l, acc

    def fetch(n):
        @pl.when(n < nb * n_groups)
        def _():
            start_group(lax.div(n, n_groups), lax.rem(n, n_groups), lax.rem(n, DEC_SLOTS))

    def arrive(n):
        slot = lax.rem(n, DEC_SLOTS)
        wait_group(slot)
        fetch(n + DEC_AHEAD)
        return slot

    first = b * n_groups

    @pl.when(b == 0)
    def _():
        for n in range(DEC_AHEAD):
            start_group(0, n, n)

    lhs_scr[0:UK_ROWS, :] = lhsw_ref[...]
    lhs_scr[UK_ROWS:UK_ROWS + QT_ROWS, :] = qt_ref[0]

    s_first = scores(arrive(first), 0)

    def body(g, carry):
        s_prev, m, l, acc = carry
        s_cur = scores(arrive(first + g), g % 2)
        m, l, acc = softmax_update(s_prev, 1 - g % 2, m, l, acc)
        return s_cur, m, l, acc

    init = (s_first, s0_ref[0], jnp.ones((MLA_HEADS, 1), F32),
            jnp.broadcast_to(cnew_ref[0], (MLA_HEADS, KV_LORA)))
    s_last, m, l, acc = lax.fori_loop(1, n_groups, body, init)
    m, l, acc = softmax_update(s_last, (n_groups - 1) % 2, m, l, acc)
    lat = acc / l
    full = _dot(lat.astype(BF16), wuv_ref[...])
    own = (lax.broadcasted_iota(jnp.int32, full.shape, 1) // V_DIM
           == lax.broadcasted_iota(jnp.int32, full.shape, 0))
    o_ref[0] = jnp.sum(jnp.where(own, full, 0.0), axis=0, keepdims=True)


def _mla_decode(page_table, lhsw, qt, qrope, s0, c_new, wuv, cache_ckv, cache_krt):
    nb, n_pages = page_table.shape
    assert n_pages % DEC_PAGES == 0 and n_pages // DEC_PAGES >= DEC_AHEAD
    per_seq = lambda shape: pl.BlockSpec((1,) + shape, lambda i, pt: (i, 0, 0))
    const = lambda a: pl.BlockSpec(a.shape, lambda i, pt: (0,) * a.ndim, pipeline_mode=pl.Buffered(1))
    grid_spec = pltpu.PrefetchScalarGridSpec(
        num_scalar_prefetch=1,
        grid=(nb,),
        in_specs=[const(lhsw), per_seq((QT_ROWS, KV_LORA)), per_seq((QT_ROWS, ROPE_DIM)),
                  per_seq((MLA_HEADS, 1)), per_seq((1, KV_LORA)), const(wuv),
                  pl.BlockSpec(memory_space=pl.ANY), pl.BlockSpec(memory_space=pl.ANY)],
        out_specs=per_seq((1, MLA_V_W)),
        scratch_shapes=[pltpu.VMEM((UK_ROWS + QT_ROWS, KV_LORA), BF16),
                        pltpu.VMEM((DEC_SLOTS, DEC_T, KV_LORA), F32),
                        pltpu.VMEM((DEC_SLOTS, ROPE_DIM, DEC_T), F32),
                        pltpu.VMEM((2, DEC_T, KV_LORA), BF16),
                        pltpu.SemaphoreType.DMA((2, DEC_SLOTS))],
    )
    return pl.pallas_call(
        _mla_decode_kernel,
        grid_spec=grid_spec,
        out_shape=jax.ShapeDtypeStruct((nb, 1, MLA_V_W), F32),
        compiler_params=_params(("arbitrary",)),
        name="mla_decode",
    )(page_table, lhsw, qt, qrope, s0, c_new, wuv, cache_ckv, cache_krt)


def _rotary_tables(pos):
    pos = pos.astype(F32)[:, None]

    def cos_sin(half):
        inv = ROPE_BASE ** (-jnp.arange(half, dtype=F32) / half)
        ang = pos * inv[None, :]
        return jnp.cos(ang), jnp.sin(ang)

    cr, sr = cos_sin(RET_DK // 2)
    cosr = jnp.concatenate([cr, cr], axis=-1)
    sinr = jnp.concatenate([-sr, sr], axis=-1)
    c, s = cos_sin(ROPE_DIM // 2)
    n = pos.shape[0]
    half = ROPE_DIM // 2
    z = lambda w: jnp.zeros((n, w), F32)
    cm = jnp.concatenate([jnp.ones((n, NOPE_DIM), F32), c, c, z(HEAD_PAD - QK_DIM)], axis=-1)
    sa = jnp.concatenate([z(NOPE_DIM), -s, z(half), z(HEAD_PAD - QK_DIM)], axis=-1)
    sb = jnp.concatenate([z(NOPE_DIM), z(half), s, z(HEAD_PAD - QK_DIM)], axis=-1)
    return cosr, sinr, cm, sa, sb


def _decay_tables(c):
    lg = jnp.log1p(-jnp.exp2(-5.0 - jnp.arange(RET_HEADS, dtype=F32)))
    idx = jnp.arange(c, dtype=F32)
    diff = idx[:, None] - idx[None, :]
    dec = jnp.where(diff[None] >= 0, jnp.exp(jnp.maximum(diff, 0.0)[None] * lg[:, None, None]), 0.0)
    qdec = jnp.exp((idx + 1.0)[None, :] * lg[:, None])[:, :, None]
    kdec = jnp.exp((c - 1.0 - idx)[None, :] * lg[:, None])[:, :, None]
    sdec = jnp.exp(c * lg)[:, None, None]
    return dec, qdec, kdec, sdec


def _pad_heads(a):
    a = jnp.pad(a, [(0, 0)] * (a.ndim - 1) + [(0, HEAD_PAD - a.shape[-1])])
    return a.reshape(a.shape[:-2] + (a.shape[-2] * HEAD_PAD,))


def _layer_weights(norm_mix, w_in, g_qa, w_qb, g_kva, w_uk, w_uv, g_qn, g_kn, w_ret_o, w_mla_o, w_out,
                   norm_ffn, w_up, w_down):
    idx = [int(i) for i in np.cumsum(SPLITS)[:-1]]
    parts = jnp.split(w_in, idx, axis=-1)
    parts[6] = jnp.pad(parts[6], ((0, 0), (NOPE_DIM, HEAD_PAD - QK_DIM)))
    pad1 = lambda g: jnp.pad(g, (0, HEAD_PAD - QK_DIM))[None, :]
    wukp = _pad_heads(w_uk)
    return {
        'norm_mix': norm_mix[None, :],
        'w_in': jnp.concatenate(parts, axis=-1).astype(BF16),
        'g_qa': g_qa[None, :],
        'w_qb': _pad_heads(w_qb.reshape(Q_LORA, MLA_HEADS, QK_DIM)).astype(BF16),
        'g_kva': g_kva[None, :],
        'w_ukp': wukp.astype(BF16),
        'w_ukp_f32': wukp,
        'w_uk_t': w_uk.transpose(2, 1, 0).reshape(UK_ROWS, KV_LORA).astype(BF16),
        'w_uv': w_uv.reshape(KV_LORA, MLA_V_W).astype(BF16),
        'w_uv_t': w_uv.reshape(KV_LORA, MLA_V_W).T.astype(BF16),
        'gq': pad1(g_qn) * (QK_DIM ** -0.5),
        'gk': pad1(g_kn),
        'w_ret_o': w_ret_o.astype(BF16),
        'w_mla_o': w_mla_o.astype(BF16),
        'w_out': w_out.astype(BF16),
        'norm_ffn': norm_ffn[None, :],
        'w_up': w_up.astype(BF16),
        'w_down': w_down.astype(BF16),
    }


PROJ_TM = 256
FINISH_TM = 512


def kernel(x_prompt, x_sample, cache_ckv, cache_krope, state_ret, page_table, meta_tokens, norm_mix, w_in,
           g_qa, w_qb, g_kva, w_uk, w_uv, g_qn, g_kn, w_ret_o, w_mla_o, w_out, norm_ffn, w_up, w_down):
    nb, seq, _ = x_prompt.shape
    ns = x_sample.shape[0]
    assert x_sample.shape[1] == 1 and norm_mix.shape[0] == 1
    past = page_table.shape[1] * PAGE_SIZE
    w = _layer_weights(norm_mix[0], w_in[0], g_qa[0], w_qb[0], g_kva[0], w_uk[0], w_uv[0], g_qn[0], g_kn[0],
                       w_ret_o[0], w_mla_o[0], w_out[0], norm_ffn[0], w_up[0], w_down[0])
    dtabs = _decay_tables(RET_CHUNK)

    mpad = RET_CHUNK - N_META
    xm = jnp.pad(meta_tokens, ((mpad, 0), (0, 0)))
    pos_m = jnp.maximum(jnp.arange(RET_CHUNK) - mpad, 0)
    rq, rk, rv, rg, _, _, _, km, vmt, c_m, kr_m = _proj(xm, _rotary_tables(pos_m), w, RET_CHUNK, RET_CHUNK,
                                                        BF16)
    zero_state = jnp.zeros((1, RET_HEADS, RET_DK, RET_DV), F32)
    _, s0 = _retention(rq[None], rk[None], rv[None], rg[None], zero_state, dtabs)

    xp = x_prompt.reshape(nb * seq, D_MODEL)
    rq, rk, rv, rg, gr, gm, q, k, vt, c_p, kr_p = _proj(
        xp, _rotary_tables(N_META + jnp.arange(seq)), w, PROJ_TM, ATT_TK, BF16)
    b3 = lambda a: a.reshape(nb, seq, a.shape[-1])
    ret_mid, st_p = _retention(b3(rq), b3(rk), b3(rv), b3(rg), s0, dtabs)
    o_mla = _attention(b3(q), b3(k), vt.reshape(nb, seq // ATT_TK, MLA_V_W, ATT_TK), km, vmt[0])
    y_prompt = _finish(xp, ret_mid.reshape(nb * seq, RET_V_W), o_mla.reshape(nb * seq, MLA_V_W), gr, gm, w,
                       FINISH_TM).reshape(nb, seq, D_MODEL)
    bcast = lambda a: jnp.broadcast_to(a[None, mpad:], (nb, N_META, a.shape[-1]))
    ckv_prompt = jnp.concatenate([bcast(c_m), b3(c_p)], axis=1)[None]
    krope_prompt = jnp.concatenate([bcast(kr_m), b3(kr_p)], axis=1)[None]

    xs = x_sample.reshape(ns, D_MODEL)
    pos_s = jnp.full((ns,), past, jnp.int32)
    rq, rk, rv, rg, gr, gm, q, k, _, c_s, kr_s = _proj(xs, _rotary_tables(pos_s), w, ns, ns, F32)
    gam = jnp.exp(jnp.log1p(-jnp.exp2(-5.0 - jnp.arange(RET_HEADS, dtype=F32))))[:, None, None]
    ret_mid, st_s = _ret_decode(rq, rk, rv, rg, gam, state_ret[0])
    qt, qg, s0_new = _absorb(q, k, w['gk'], w['w_ukp_f32'])
    qt = jnp.pad(qt.transpose(1, 0, 2), ((0, 0), (0, QT_ROWS - MLA_HEADS), (0, 0))).astype(BF16)
    q_rope = qg.reshape(ns, MLA_HEADS, HEAD_PAD)[:, :, NOPE_DIM:QK_DIM]
    q_rope = jnp.pad(q_rope, ((0, 0), (0, QT_ROWS - MLA_HEADS), (0, 0))).astype(BF16)
    o_mla = _mla_decode(page_table, w['w_uk_t'], qt, q_rope, s0_new[:, :MLA_HEADS, None], c_s[:, None, :],
                        w['w_uv'], cache_ckv, jnp.swapaxes(cache_krope, 2, 3))
    y_sample = _finish(xs, ret_mid, o_mla.reshape(ns, MLA_V_W), gr, gm, w, ns).reshape(ns, 1, D_MODEL)

    return (y_prompt, y_sample, ckv_prompt, krope_prompt, st_p[None],
            c_s.reshape(1, ns, 1, KV_LORA), kr_s.reshape(1, ns, 1, ROPE_DIM), st_s[None])
```

```python
import numpy as np
import jax
import jax.numpy as jnp
from jax import lax
from jax.experimental import pallas as pl
from jax.experimental.pallas import tpu as pltpu

D_MODEL = 1024
N_META = 16
RET_HEADS = 4
RET_DK = 128
RET_DV = 256
RET_CHUNK = 128
MLA_HEADS = 8
Q_LORA = 384
KV_LORA = 256
NOPE_DIM = 64
ROPE_DIM = 32
QK_DIM = NOPE_DIM + ROPE_DIM
V_DIM = 64
D_FF = 4 * D_MODEL
PAGE_SIZE = 128
ROPE_BASE = 10000.0
EPS = 1e-6
LOG2_E = float(np.log2(np.e))

RET_QK_W = RET_HEADS * RET_DK
RET_V_W = RET_HEADS * RET_DV
MLA_V_W = MLA_HEADS * V_DIM
SPLITS = (RET_QK_W, RET_QK_W, RET_V_W, RET_V_W, Q_LORA, KV_LORA, ROPE_DIM, D_MODEL, D_MODEL)

LANES = 128
HEAD_PAD = LANES
MLA_PAD_W = MLA_HEADS * HEAD_PAD
_SEG_W = (RET_QK_W, RET_QK_W, RET_V_W, RET_V_W, Q_LORA, KV_LORA, LANES, D_MODEL, D_MODEL)
_SEG_ORDER = (4, 5, 6, 0, 1, 2, 3, 7, 8)
_SEG_START = dict(zip(_SEG_ORDER, np.cumsum([0] + [_SEG_W[i] for i in _SEG_ORDER[:-1]]).tolist()))
_SEG_O = tuple(_SEG_START[i] for i in range(len(_SEG_W)))
IN_PAD_W = sum(_SEG_W)
VMEM_LIMIT = 56 * 1024 * 1024

F32 = jnp.float32
BF16 = jnp.bfloat16
_NT = (((1,), (1,)), ((), ()))
_TN = (((0,), (0,)), ((), ()))


def _dot(a, b):
    return jnp.dot(a, b, preferred_element_type=F32)


def _dot_nt(a, b):
    return lax.dot_general(a, b, _NT, preferred_element_type=F32)


def _const_spec(shape):
    zeros = (0,) * len(shape)
    return pl.BlockSpec(shape, lambda *_: zeros, pipeline_mode=pl.Buffered(1))


def _params(sem):
    return pltpu.CompilerParams(dimension_semantics=sem, vmem_limit_bytes=VMEM_LIMIT)


def _proj_kernel(x_ref, cosr_ref, sinr_ref, cm_ref, sa_ref, sb_ref, nmix_ref, win_ref, gqa_ref,
                 wqb_ref, gkva_ref, wukp_ref, wuvt_ref, gq_ref, gk_ref,
                 rq_ref, rk_ref, rv_ref, rg_ref, gr_ref, gm_ref, q_ref, k_ref, vt_ref, c_ref, kr_ref):
    x = x_ref[...]
    xn = x * lax.rsqrt(jnp.mean(x * x, axis=-1, keepdims=True) + EPS) * nmix_ref[...]
    xb = xn.astype(BF16)

    def seg(i):
        return _dot(xb, win_ref[:, _SEG_O[i]:_SEG_O[i] + _SEG_W[i]])

    cosr, sinr = cosr_ref[...], sinr_ref[...]
    cm, sa, sb = cm_ref[...], sa_ref[...], sb_ref[...]

    def ret_rotary(z, out_ref, scale):
        for h in range(RET_HEADS):
            zh = z[:, h * RET_DK:(h + 1) * RET_DK]
            r = zh * cosr + pltpu.roll(zh, RET_DK // 2, 1) * sinr
            if scale is not None:
                r = r * scale
            out_ref[:, h * RET_DK:(h + 1) * RET_DK] = r.astype(out_ref.dtype)

    def mla_rotary(zh):
        half = ROPE_DIM // 2
        return zh * cm + pltpu.roll(zh, HEAD_PAD - half, 1) * sa + pltpu.roll(zh, half, 1) * sb

    def head_norm(zh, g):
        ss = jnp.sum(zh * zh, axis=-1, keepdims=True)
        return zh * lax.rsqrt(ss * (1.0 / QK_DIM) + EPS) * g

    qa = seg(4)
    qn = qa * lax.rsqrt(jnp.mean(qa * qa, axis=-1, keepdims=True) + EPS) * gqa_ref[...]
    ckv = seg(5)
    c = ckv * lax.rsqrt(jnp.mean(ckv * ckv, axis=-1, keepdims=True) + EPS) * gkva_ref[...]
    c_ref[...] = c
    cb = c.astype(BF16)
    krp = mla_rotary(seg(6))
    kr_ref[...] = krp[:, NOPE_DIM:QK_DIM]

    ret_rotary(seg(0), rq_ref, None)
    ret_rotary(seg(1), rk_ref, RET_DK ** -0.5)
    rv_ref[...] = seg(2).astype(rv_ref.dtype)
    rg_ref[...] = seg(3).astype(rg_ref.dtype)
    gr_ref[...] = seg(7).astype(gr_ref.dtype)
    gm_ref[...] = seg(8).astype(gm_ref.dtype)

    qp = _dot(qn.astype(BF16), wqb_ref[...])
    gq = gq_ref[...]
    for h in range(MLA_HEADS):
        qh = mla_rotary(qp[:, h * HEAD_PAD:(h + 1) * HEAD_PAD])
        q_ref[:, h * HEAD_PAD:(h + 1) * HEAD_PAD] = head_norm(qh, gq).astype(q_ref.dtype)

    kp = _dot(cb, wukp_ref[...])
    gk = gk_ref[...]
    for h in range(MLA_HEADS):
        kh = kp[:, h * HEAD_PAD:(h + 1) * HEAD_PAD] + krp
        k_ref[:, h * HEAD_PAD:(h + 1) * HEAD_PAD] = head_norm(kh, gk).astype(k_ref.dtype)
    vt_w = vt_ref.shape[-1]
    for t in range(vt_ref.shape[0]):
        vt_ref[t] = _dot_nt(wuvt_ref[...], cb[t * vt_w:(t + 1) * vt_w]).astype(vt_ref.dtype)


def _proj(x, tabs, w, tm, vt_w, mid_dtype):
    rows = x.shape[0]
    period = tabs[0].shape[0] // tm
    row = lambda i: (i, 0)
    tab_spec = pl.BlockSpec((tm, LANES), lambda i: (i % period, 0))
    widths = (RET_QK_W, RET_QK_W, RET_V_W, RET_V_W, D_MODEL, D_MODEL, MLA_PAD_W, MLA_PAD_W)
    out_shape = [jax.ShapeDtypeStruct((rows, n), mid_dtype) for n in widths]
    out_shape += [jax.ShapeDtypeStruct((rows // vt_w, MLA_V_W, vt_w), mid_dtype),
                  jax.ShapeDtypeStruct((rows, KV_LORA), F32), jax.ShapeDtypeStruct((rows, ROPE_DIM), F32)]
    out_specs = [pl.BlockSpec((tm, n), row) for n in widths]
    out_specs += [pl.BlockSpec((tm // vt_w, MLA_V_W, vt_w), lambda i: (i, 0, 0)),
                  pl.BlockSpec((tm, KV_LORA), row), pl.BlockSpec((tm, ROPE_DIM), row)]
    consts = (w['norm_mix'], w['w_in'], w['g_qa'], w['w_qb'], w['g_kva'], w['w_ukp'], w['w_uv_t'],
              w['gq'], w['gk'])
    return pl.pallas_call(
        _proj_kernel,
        grid=(rows // tm,),
        in_specs=[pl.BlockSpec((tm, D_MODEL), row)] + [tab_spec] * 5 + [_const_spec(a.shape) for a in consts],
        out_specs=out_specs,
        out_shape=out_shape,
        compiler_params=_params(("parallel",)),
        name="proj",
    )(x, *tabs, *consts)


def _retention_kernel(q_ref, k_ref, v_ref, rg_ref, dec_ref, qdec_ref, kdec_ref, sdec_ref, s0_ref,
                      o_ref, st_ref, s_scr):
    ci = pl.program_id(1)

    @pl.when(ci == 0)
    def _():
        for r in range(s_scr.shape[0]):
            s_scr[r] = s0_ref[0]

    for r in range(s_scr.shape[0]):
        for h in range(RET_HEADS):
            qh = q_ref[r, :, h * RET_DK:(h + 1) * RET_DK]
            kh = k_ref[r, :, h * RET_DK:(h + 1) * RET_DK]
            vh = v_ref[r, :, h * RET_DV:(h + 1) * RET_DV]
            scores = _dot_nt(qh, kh) * dec_ref[h]
            o = _dot(scores.astype(BF16), vh)
            s_prev = s_scr[r, h]
            o = o + _dot(qh, s_prev.astype(BF16)) * qdec_ref[h]
            kd = (kh.astype(F32) * kdec_ref[h]).astype(BF16)
            s_scr[r, h] = sdec_ref[h] * s_prev + lax.dot_general(kd, vh, _TN, preferred_element_type=F32)
            on = o * lax.rsqrt(jnp.mean(o * o, axis=-1, keepdims=True) + EPS)
            g = rg_ref[r, :, h * RET_DV:(h + 1) * RET_DV].astype(F32)
            o_ref[r, :, h * RET_DV:(h + 1) * RET_DV] = (g * jax.nn.sigmoid(g) * on).astype(o_ref.dtype)

    @pl.when(ci == pl.num_programs(1) - 1)
    def _():
        st_ref[...] = s_scr[...]


RET_BATCH = 4


def _retention(rq, rk, rv, rg, s0, dtabs):
    nb, s = rq.shape[:2]
    c = RET_CHUNK
    rb = min(nb, RET_BATCH)
    blk = lambda n: pl.BlockSpec((rb, c, n), lambda b, i: (b, i, 0))
    st_shape = (RET_HEADS, RET_DK, RET_DV)
    return pl.pallas_call(
        _retention_kernel,
        grid=(nb // rb, s // c),
        in_specs=[blk(RET_QK_W), blk(RET_QK_W), blk(RET_V_W), blk(RET_V_W)]
        + [_const_spec(t.shape) for t in dtabs] + [_const_spec((1,) + st_shape)],
        out_specs=[blk(RET_V_W), pl.BlockSpec((rb,) + st_shape, lambda b, i: (b, 0, 0, 0))],
        out_shape=[jax.ShapeDtypeStruct((nb, s, RET_V_W), BF16),
                   jax.ShapeDtypeStruct((nb,) + st_shape, F32)],
        scratch_shapes=[pltpu.VMEM((rb,) + st_shape, F32)],
        compiler_params=_params(("parallel", "arbitrary")),
        name="retention",
    )(rq, rk, rv, rg, *dtabs, s0)


ATT_TQ = 512
ATT_TK = 256


def _attention_kernel(q_ref, k_ref, vt_ref, km_ref, vmt_ref, o_ref, qt_scr, s_scr, m_scr, l_scr, acc_scr):
    qi = pl.program_id(1)
    meta_rows = km_ref.shape[0]
    hs = lambda h: slice(h * HEAD_PAD, (h + 1) * HEAD_PAD)
    vs = lambda h: slice(h * V_DIM, (h + 1) * V_DIM)

    qt_scr[...] = q_ref[...].T
    m_scr[...] = jnp.full(m_scr.shape, -jnp.inf, F32)
    l_scr[...] = jnp.zeros(l_scr.shape, F32)
    acc_scr[...] = jnp.zeros(acc_scr.shape, F32)

    def update(n, keys, values_t, mask):
        for h in range(MLA_HEADS):
            s_scr[h, 0:n, :] = _dot(keys(h), qt_scr[hs(h), :])
        for h in range(MLA_HEADS):
            s = s_scr[h, 0:n, :]
            if mask is not None:
                s = jnp.where(mask, s, -jnp.inf)
            m_old = m_scr[h]
            m_new = jnp.maximum(m_old, jnp.max(s, axis=0, keepdims=True))
            alpha = jnp.exp2(m_old - m_new)
            p = jnp.exp2(s - m_new)
            m_scr[h] = m_new
            l_scr[h] = alpha * l_scr[h] + jnp.sum(p, axis=0, keepdims=True)
            acc_scr[vs(h), :] = alpha * acc_scr[vs(h), :] + _dot(values_t(h), p.astype(BF16))

    meta_ok = lax.broadcasted_iota(jnp.int32, (meta_rows, ATT_TQ), 0) >= meta_rows - N_META
    update(meta_rows, lambda h: km_ref[:, hs(h)], lambda h: vmt_ref[vs(h), :], meta_ok)

    def key_tile(j, mask):
        start = pl.multiple_of(j * ATT_TK, ATT_TK)
        update(ATT_TK, lambda h: k_ref[pl.ds(start, ATT_TK), hs(h)], lambda h: vt_ref[j, vs(h), :], mask)

    def body(j, carry):
        key_tile(j, None)
        return carry

    first_diag = qi * (ATT_TQ // ATT_TK)
    lax.fori_loop(0, first_diag, body, 0)
    key_pos = lax.broadcasted_iota(jnp.int32, (ATT_TK, ATT_TQ), 0)
    query_pos = lax.broadcasted_iota(jnp.int32, (ATT_TK, ATT_TQ), 1)
    for d in range(ATT_TQ // ATT_TK):
        key_tile(first_diag + d, key_pos + d * ATT_TK <= query_pos)

    for h in range(MLA_HEADS):
        acc_scr[vs(h), :] = acc_scr[vs(h), :] / l_scr[h]
    o_ref[...] = acc_scr[...].T.astype(o_ref.dtype)


def _attention(q, k, vt, km, vmt):
    nb, s = q.shape[:2]
    return pl.pallas_call(
        _attention_kernel,
        grid=(nb, s // ATT_TQ),
        in_specs=[pl.BlockSpec((None, ATT_TQ, MLA_PAD_W), lambda b, i: (b, i, 0)),
                  pl.BlockSpec((None, s, MLA_PAD_W), lambda b, i: (b, 0, 0)),
                  pl.BlockSpec((None, s // ATT_TK, MLA_V_W, ATT_TK), lambda b, i: (b, 0, 0, 0)),
                  _const_spec(km.shape), _const_spec(vmt.shape)],
        out_specs=pl.BlockSpec((None, ATT_TQ, MLA_V_W), lambda b, i: (b, i, 0)),
        out_shape=jax.ShapeDtypeStruct((nb, s, MLA_V_W), BF16),
        scratch_shapes=[pltpu.VMEM((MLA_PAD_W, ATT_TQ), BF16),
                        pltpu.VMEM((MLA_HEADS, ATT_TK, ATT_TQ), F32),
                        pltpu.VMEM((MLA_HEADS, 1, ATT_TQ), F32), pltpu.VMEM((MLA_HEADS, 1, ATT_TQ), F32),
                        pltpu.VMEM((MLA_V_W, ATT_TQ), F32)],
        compiler_params=_params(("parallel", "arbitrary")),
        name="attention",
    )(q, k, vt, km, vmt)


FF_BLOCK = 1024


def _finish_kernel(x_ref, ret_ref, mla_ref, gr_ref, gm_ref, wro_ref, wmo_ref, wout_ref, nffn_ref,
                   wup_ref, wdn_ref, y_ref):
    r = _dot(ret_ref[...].astype(BF16), wro_ref[...])
    m = _dot(mla_ref[...].astype(BF16), wmo_ref[...])
    mixed = jax.nn.sigmoid(gr_ref[...].astype(F32)) * r + jax.nn.sigmoid(gm_ref[...].astype(F32)) * m
    h = x_ref[...] + _dot(mixed.astype(BF16), wout_ref[...])
    hn = h * lax.rsqrt(jnp.mean(h * h, axis=-1, keepdims=True) + EPS) * nffn_ref[...]
    hb = hn.astype(BF16)
    y = h
    for f in range(D_FF // FF_BLOCK):
        u = jnp.maximum(_dot(hb, wup_ref[:, f * FF_BLOCK:(f + 1) * FF_BLOCK]), 0.0)
        y = y + _dot((u * u).astype(BF16), wdn_ref[f * FF_BLOCK:(f + 1) * FF_BLOCK, :])
    y_ref[...] = y


def _finish(x, ret_mid, o_mla, gr, gm, w, tm):
    rows = x.shape[0]
    row = lambda i: (i, 0)
    consts = (w['w_ret_o'], w['w_mla_o'], w['w_out'], w['norm_ffn'], w['w_up'], w['w_down'])
    return pl.pallas_call(
        _finish_kernel,
        grid=(rows // tm,),
        in_specs=[pl.BlockSpec((tm, D_MODEL), row), pl.BlockSpec((tm, RET_V_W), row),
                  pl.BlockSpec((tm, MLA_V_W), row), pl.BlockSpec((tm, D_MODEL), row),
                  pl.BlockSpec((tm, D_MODEL), row)] + [_const_spec(a.shape) for a in consts],
        out_specs=pl.BlockSpec((tm, D_MODEL), row),
        out_shape=jax.ShapeDtypeStruct((rows, D_MODEL), F32),
        compiler_params=_params(("parallel",)),
        name="finish",
    )(x, ret_mid, o_mla, gr, gm, *consts)


RET_DEC_GROUP = 8


def _column(row_vec, eye):
    return jnp.sum(jnp.where(eye, row_vec, 0.0), axis=-1, keepdims=True)


def _ret_decode_kernel(q_ref, k_ref, v_ref, rg_ref, gam_ref, s_ref, o_ref, sn_ref):
    eye = (lax.broadcasted_iota(jnp.int32, (RET_DK, RET_DK), 0)
           == lax.broadcasted_iota(jnp.int32, (RET_DK, RET_DK), 1))
    for b in range(RET_DEC_GROUP):
        for h in range(RET_HEADS):
            k_col = _column(k_ref[b:b + 1, h * RET_DK:(h + 1) * RET_DK].astype(F32), eye)
            q_col = _column(q_ref[b:b + 1, h * RET_DK:(h + 1) * RET_DK].astype(F32), eye)
            v_row = v_ref[b:b + 1, h * RET_DV:(h + 1) * RET_DV].astype(F32)
            gam = gam_ref[h]
            s_prev = s_ref[b, h]
            kv = k_col * v_row
            sn_ref[b, h] = gam * s_prev + kv
            o = jnp.sum(q_col * kv, axis=0, keepdims=True) + gam * jnp.sum(q_col * s_prev, axis=0, keepdims=True)
            on = o * lax.rsqrt(jnp.mean(o * o, axis=-1, keepdims=True) + EPS)
            g = rg_ref[b:b + 1, h * RET_DV:(h + 1) * RET_DV].astype(F32)
            o_ref[b:b + 1, h * RET_DV:(h + 1) * RET_DV] = (g * jax.nn.sigmoid(g) * on).astype(o_ref.dtype)


def _ret_decode(rq, rk, rv, rg, gam, state):
    nb = rq.shape[0]
    g = RET_DEC_GROUP
    row = lambda i: (i, 0)
    st_spec = pl.BlockSpec((g, RET_HEADS, RET_DK, RET_DV), lambda i: (i, 0, 0, 0))
    return pl.pallas_call(
        _ret_decode_kernel,
        grid=(nb // g,),
        in_specs=[pl.BlockSpec((g, RET_QK_W), row), pl.BlockSpec((g, RET_QK_W), row),
                  pl.BlockSpec((g, RET_V_W), row), pl.BlockSpec((g, RET_V_W), row),
                  _const_spec(gam.shape), st_spec],
        out_specs=[pl.BlockSpec((g, RET_V_W), row), st_spec],
        out_shape=[jax.ShapeDtypeStruct((nb, RET_V_W), F32), jax.ShapeDtypeStruct(state.shape, F32)],
        compiler_params=_params(("parallel",)),
        name="ret_decode",
    )(rq, rk, rv, rg, gam, state)


def _absorb_kernel(q_ref, k_ref, gk_ref, wukp_ref, qt_ref, qg_ref, s0_ref):
    q = q_ref[...]
    lane = lax.broadcasted_iota(jnp.int32, (q.shape[0], LANES), 1)
    s0 = jnp.zeros((q.shape[0], LANES), F32)
    gk = gk_ref[...]
    for h in range(MLA_HEADS):
        sl = slice(h * HEAD_PAD, (h + 1) * HEAD_PAD)
        qh = q[:, sl]
        s0 = s0 + jnp.where(lane == h, jnp.sum(qh * k_ref[:, sl], axis=-1, keepdims=True), 0.0)
        qg = qh * gk
        qg_ref[:, sl] = qg
        qt_ref[h] = lax.dot_general(qg, wukp_ref[:, sl], _NT, precision=lax.Precision.HIGHEST,
                                    preferred_element_type=F32)
    s0_ref[...] = s0


def _absorb(q, k, gk, wukp):
    nb = q.shape[0]
    return pl.pallas_call(
        _absorb_kernel,
        out_shape=[jax.ShapeDtypeStruct((MLA_HEADS, nb, KV_LORA), F32),
                   jax.ShapeDtypeStruct((nb, MLA_PAD_W), F32),
                   jax.ShapeDtypeStruct((nb, LANES), F32)],
        compiler_params=pltpu.CompilerParams(vmem_limit_bytes=VMEM_LIMIT),
        name="absorb",
    )(q, k, gk, wukp)


DEC_PAGES = 32
DEC_T = DEC_PAGES * PAGE_SIZE
DEC_SUB = 512
UK_ROWS = MLA_HEADS * NOPE_DIM
QT_ROWS = 16
DEC_AHEAD = 2
DEC_SLOTS = DEC_AHEAD + 1


def _mla_decode_kernel(pt_ref, lhsw_ref, qt_ref, qrope_ref, s0_ref, cnew_ref, wuv_ref, ckv_hbm, krt_hbm,
                       o_ref, lhs_scr, cbuf, kbuf, cb_scr, big_scr, sem):
    b = pl.program_id(0)
    nb = pl.num_programs(0)
    n_groups = pt_ref.shape[1] // DEC_PAGES

    def page_copies(seq, g, slot):
        copies = []
        for i in range(DEC_PAGES):
            page = pt_ref[seq, g * DEC_PAGES + i]
            tok = pl.ds(i * PAGE_SIZE, PAGE_SIZE)
            copies.append(pltpu.make_async_copy(ckv_hbm.at[0, page], cbuf.at[slot, tok], sem.at[0, slot]))
            copies.append(pltpu.make_async_copy(krt_hbm.at[0, page], kbuf.at[slot, :, tok], sem.at[1, slot]))
        return copies

    def start_group(seq, g, slot):
        for cp in page_copies(seq, g, slot):
            cp.start()

    def wait_group(slot):
        for cp in page_copies(0, 0, slot):
            cp.wait()

    def scores(slot, cb_slot):
        cb = cbuf[slot].astype(BF16)
        cb_scr[cb_slot] = cb
        big_scr[...] = _dot_nt(lhs_scr[...], cb)
        rope = _dot(qrope_ref[0], kbuf[slot].astype(BF16))
        pieces = []
        for t in range(DEC_T // DEC_SUB):
            tok = pl.ds(t * DEC_SUB, DEC_SUB)
            sq = [big_scr[d * MLA_HEADS:(d + 1) * MLA_HEADS, tok] for d in range(NOPE_DIM)]
            sq = [x * x for x in sq]
            while len(sq) > 1:
                sq = [sq[i] + sq[i + 1] for i in range(0, len(sq), 2)]
            krt = kbuf[slot, :, tok]
            kk = krt * krt
            kk = (kk[0:8] + kk[8:16]) + (kk[16:24] + kk[24:32])
            ss = sq[0] + jnp.sum(kk, axis=0, keepdims=True)
            num = big_scr[UK_ROWS:UK_ROWS + MLA_HEADS, tok] + rope[0:MLA_HEADS, t * DEC_SUB:(t + 1) * DEC_SUB]
            pieces.append(num * lax.rsqrt(ss * (1.0 / QK_DIM) + EPS))
        return jnp.concatenate(pieces, axis=-1)

    def softmax_update(s, slot, m, l, acc):
        m_new = jnp.maximum(m, jnp.max(s, axis=-1, keepdims=True))
        alpha = jnp.exp2(m - m_new)
        p = jnp.exp2(s - m_new)
        l = alpha * l + jnp.sum(p, axis=-1, keepdims=True)
        acc = alpha * acc + _dot(p.astype(BF16), cb_scr[slot])
        return m_new, l, acc

    last = nb * n_groups - 1

    def fetch(n):
        src = jnp.minimum(n, last)
        start_group(lax.div(src, n_groups), lax.rem(src, n_groups), lax.rem(n, DEC_SLOTS))

    def arrive(n):
        slot = lax.rem(n, DEC_SLOTS)
        wait_group(slot)
        return slot

    first = b * n_groups

    @pl.when(b == 0)
    def _():
        for n in range(DEC_AHEAD):
            start_group(0, n, n)

    lhs_scr[0:UK_ROWS, :] = lhsw_ref[...]
    lhs_scr[UK_ROWS:UK_ROWS + QT_ROWS, :] = qt_ref[0]

    s_first = scores(arrive(first), 0)
    fetch(first + DEC_AHEAD)

    def body(g, carry):
        s_prev, m, l, acc = carry
        s_cur = scores(arrive(first + g), g % 2)
        m, l, acc = softmax_update(s_prev, 1 - g % 2, m, l, acc)
        fetch(first + g + DEC_AHEAD)
        return s_cur, m, l, acc

    init = (s_first, s0_ref[0], jnp.ones((MLA_HEADS, 1), F32),
            jnp.broadcast_to(cnew_ref[0], (MLA_HEADS, KV_LORA)))
    s_last, m, l, acc = lax.fori_loop(1, n_groups, body, init)
    m, l, acc = softmax_update(s_last, (n_groups - 1) % 2, m, l, acc)

    @pl.when(b == nb - 1)
    def _():
        for n in range(1, DEC_AHEAD + 1):
            wait_group(lax.rem(last + n, DEC_SLOTS))

    lat = acc / l
    full = _dot(lat.astype(BF16), wuv_ref[...])
    own = (lax.broadcasted_iota(jnp.int32, full.shape, 1) // V_DIM
           == lax.broadcasted_iota(jnp.int32, full.shape, 0))
    o_ref[0] = jnp.sum(jnp.where(own, full, 0.0), axis=0, keepdims=True)


def _mla_decode(page_table, lhsw, qt, qrope, s0, c_new, wuv, cache_ckv, cache_krt):
    nb, n_pages = page_table.shape
    assert n_pages % DEC_PAGES == 0 and n_pages // DEC_PAGES >= DEC_AHEAD
    per_seq = lambda shape: pl.BlockSpec((1,) + shape, lambda i, pt: (i, 0, 0))
    const = lambda a: pl.BlockSpec(a.shape, lambda i, pt: (0,) * a.ndim, pipeline_mode=pl.Buffered(1))
    grid_spec = pltpu.PrefetchScalarGridSpec(
        num_scalar_prefetch=1,
        grid=(nb,),
        in_specs=[const(lhsw), per_seq((QT_ROWS, KV_LORA)), per_seq((QT_ROWS, ROPE_DIM)),
                  per_seq((MLA_HEADS, 1)), per_seq((1, KV_LORA)), const(wuv),
                  pl.BlockSpec(memory_space=pl.ANY), pl.BlockSpec(memory_space=pl.ANY)],
        out_specs=per_seq((1, MLA_V_W)),
        scratch_shapes=[pltpu.VMEM((UK_ROWS + QT_ROWS, KV_LORA), BF16),
                        pltpu.VMEM((DEC_SLOTS, DEC_T, KV_LORA), F32),
                        pltpu.VMEM((DEC_SLOTS, ROPE_DIM, DEC_T), F32),
                        pltpu.VMEM((2, DEC_T, KV_LORA), BF16),
                        pltpu.VMEM((UK_ROWS + QT_ROWS, DEC_T), F32),
                        pltpu.SemaphoreType.DMA((2, DEC_SLOTS))],
    )
    return pl.pallas_call(
        _mla_decode_kernel,
        grid_spec=grid_spec,
        out_shape=jax.ShapeDtypeStruct((nb, 1, MLA_V_W), F32),
        compiler_params=_params(("arbitrary",)),
        name="mla_decode",
    )(page_table, lhsw, qt, qrope, s0, c_new, wuv, cache_ckv, cache_krt)


def _rotary_tables(pos):
    pos = pos.astype(F32)[:, None]

    def cos_sin(half):
        inv = ROPE_BASE ** (-jnp.arange(half, dtype=F32) / half)
        ang = pos * inv[None, :]
        return jnp.cos(ang), jnp.sin(ang)

    cr, sr = cos_sin(RET_DK // 2)
    cosr = jnp.concatenate([cr, cr], axis=-1)
    sinr = jnp.concatenate([-sr, sr], axis=-1)
    c, s = cos_sin(ROPE_DIM // 2)
    n = pos.shape[0]
    half = ROPE_DIM // 2
    z = lambda w: jnp.zeros((n, w), F32)
    cm = jnp.concatenate([jnp.ones((n, NOPE_DIM), F32), c, c, z(HEAD_PAD - QK_DIM)], axis=-1)
    sa = jnp.concatenate([z(NOPE_DIM), -s, z(half), z(HEAD_PAD - QK_DIM)], axis=-1)
    sb = jnp.concatenate([z(NOPE_DIM), z(half), s, z(HEAD_PAD - QK_DIM)], axis=-1)
    return cosr, sinr, cm, sa, sb


def _decay_tables(c):
    lg = jnp.log1p(-jnp.exp2(-5.0 - jnp.arange(RET_HEADS, dtype=F32)))
    idx = jnp.arange(c, dtype=F32)
    diff = idx[:, None] - idx[None, :]
    dec = jnp.where(diff[None] >= 0, jnp.exp(jnp.maximum(diff, 0.0)[None] * lg[:, None, None]), 0.0)
    qdec = jnp.exp((idx + 1.0)[None, :] * lg[:, None])[:, :, None]
    kdec = jnp.exp((c - 1.0 - idx)[None, :] * lg[:, None])[:, :, None]
    sdec = jnp.exp(c * lg)[:, None, None]
    return dec, qdec, kdec, sdec


def _pad_heads(a):
    a = jnp.pad(a, [(0, 0)] * (a.ndim - 1) + [(0, HEAD_PAD - a.shape[-1])])
    return a.reshape(a.shape[:-2] + (a.shape[-2] * HEAD_PAD,))


def _layer_weights(norm_mix, w_in, g_qa, w_qb, g_kva, w_uk, w_uv, g_qn, g_kn, w_ret_o, w_mla_o, w_out,
                   norm_ffn, w_up, w_down):
    idx = [int(i) for i in np.cumsum(SPLITS)[:-1]]
    parts = jnp.split(w_in, idx, axis=-1)
    parts[6] = jnp.pad(parts[6], ((0, 0), (NOPE_DIM, HEAD_PAD - QK_DIM)))
    pad1 = lambda g: jnp.pad(g, (0, HEAD_PAD - QK_DIM))[None, :]
    wukp = _pad_heads(w_uk)
    return {
        'norm_mix': norm_mix[None, :],
        'w_in': jnp.concatenate([parts[i] for i in _SEG_ORDER], axis=-1).astype(BF16),
        'g_qa': g_qa[None, :],
        'w_qb': _pad_heads(w_qb.reshape(Q_LORA, MLA_HEADS, QK_DIM)).astype(BF16),
        'g_kva': g_kva[None, :],
        'w_ukp': wukp.astype(BF16),
        'w_ukp_f32': wukp,
        'w_uk_t': w_uk.transpose(2, 1, 0).reshape(UK_ROWS, KV_LORA).astype(BF16),
        'w_uv': w_uv.reshape(KV_LORA, MLA_V_W).astype(BF16),
        'w_uv_t': w_uv.reshape(KV_LORA, MLA_V_W).T.astype(BF16),
        'gq': pad1(g_qn) * (QK_DIM ** -0.5 * LOG2_E),
        'gk': pad1(g_kn),
        'w_ret_o': w_ret_o.astype(BF16),
        'w_mla_o': w_mla_o.astype(BF16),
        'w_out': w_out.astype(BF16),
        'norm_ffn': norm_ffn[None, :],
        'w_up': w_up.astype(BF16),
        'w_down': w_down.astype(BF16),
    }


PROJ_TM = 256
FINISH_TM = 512


def kernel(x_prompt, x_sample, cache_ckv, cache_krope, state_ret, page_table, meta_tokens, norm_mix, w_in,
           g_qa, w_qb, g_kva, w_uk, w_uv, g_qn, g_kn, w_ret_o, w_mla_o, w_out, norm_ffn, w_up, w_down):
    nb, seq, _ = x_prompt.shape
    ns = x_sample.shape[0]
    assert x_sample.shape[1] == 1 and norm_mix.shape[0] == 1
    past = page_table.shape[1] * PAGE_SIZE
    w = _layer_weights(norm_mix[0], w_in[0], g_qa[0], w_qb[0], g_kva[0], w_uk[0], w_uv[0], g_qn[0], g_kn[0],
                       w_ret_o[0], w_mla_o[0], w_out[0], norm_ffn[0], w_up[0], w_down[0])
    dtabs = _decay_tables(RET_CHUNK)

    mpad = RET_CHUNK - N_META
    xm = jnp.pad(meta_tokens, ((mpad, 0), (0, 0)))
    pos_m = jnp.maximum(jnp.arange(RET_CHUNK) - mpad, 0)
    rq, rk, rv, rg, _, _, _, km, vmt, c_m, kr_m = _proj(xm, _rotary_tables(pos_m), w, RET_CHUNK, RET_CHUNK,
                                                        BF16)
    zero_state = jnp.zeros((1, RET_HEADS, RET_DK, RET_DV), F32)
    _, s0 = _retention(rq[None], rk[None], rv[None], rg[None], zero_state, dtabs)

    xp = x_prompt.reshape(nb * seq, D_MODEL)
    rq, rk, rv, rg, gr, gm, q, k, vt, c_p, kr_p = _proj(
        xp, _rotary_tables(N_META + jnp.arange(seq)), w, PROJ_TM, ATT_TK, BF16)
    b3 = lambda a: a.reshape(nb, seq, a.shape[-1])
    ret_mid, st_p = _retention(b3(rq), b3(rk), b3(rv), b3(rg), s0, dtabs)
    o_mla = _attention(b3(q), b3(k), vt.reshape(nb, seq // ATT_TK, MLA_V_W, ATT_TK), km, vmt[0])
    y_prompt = _finish(xp, ret_mid.reshape(nb * seq, RET_V_W), o_mla.reshape(nb * seq, MLA_V_W), gr, gm, w,
                       FINISH_TM).reshape(nb, seq, D_MODEL)
    bcast = lambda a: jnp.broadcast_to(a[None, mpad:], (nb, N_META, a.shape[-1]))
    ckv_prompt = jnp.concatenate([bcast(c_m), b3(c_p)], axis=1)[None]
    krope_prompt = jnp.concatenate([bcast(kr_m), b3(kr_p)], axis=1)[None]

    xs = x_sample.reshape(ns, D_MODEL)
    pos_s = jnp.full((ns,), past, jnp.int32)
    rq, rk, rv, rg, gr, gm, q, k, _, c_s, kr_s = _proj(xs, _rotary_tables(pos_s), w, ns, ns, F32)
    gam = jnp.exp(jnp.log1p(-jnp.exp2(-5.0 - jnp.arange(RET_HEADS, dtype=F32))))[:, None, None]
    ret_mid, st_s = _ret_decode(rq, rk, rv, rg, gam, state_ret[0])
    qt, qg, s0_new = _absorb(q, k, w['gk'], w['w_ukp_f32'])
    qt = jnp.pad(qt.transpose(1, 0, 2), ((0, 0), (0, QT_ROWS - MLA_HEADS), (0, 0))).astype(BF16)
    q_rope = qg.reshape(ns, MLA_HEADS, HEAD_PAD)[:, :, NOPE_DIM:QK_DIM]
    q_rope = jnp.pad(q_rope, ((0, 0), (0, QT_ROWS - MLA_HEADS), (0, 0))).astype(BF16)
    o_mla = _mla_decode(page_table, w['w_uk_t'], qt, q_rope, s0_new[:, :MLA_HEADS, None], c_s[:, None, :],
                        w['w_uv'], cache_ckv, jnp.swapaxes(cache_krope, 2, 3))
    y_sample = _finish(xs, ret_mid, o_mla.reshape(ns, MLA_V_W), gr, gm, w, ns).reshape(ns, 1, D_MODEL)

    return (y_prompt, y_sample, ckv_prompt, krope_prompt, st_p[None],
            c_s.reshape(1, ns, 1, KV_LORA), kr_s.reshape(1, ns, 1, ROPE_DIM), st_s[None])
```

```python
import functools

import numpy as np
import jax
import jax.numpy as jnp
from jax import lax
from jax.experimental import pallas as pl
from jax.experimental.pallas import tpu as pltpu

D_MODEL = 1024
N_META = 16
RET_HEADS = 4
RET_DK = 128
RET_DV = 256
RET_CHUNK = 128
MLA_HEADS = 8
Q_LORA = 384
KV_LORA = 256
NOPE_DIM = 64
ROPE_DIM = 32
QK_DIM = NOPE_DIM + ROPE_DIM
V_DIM = 64
D_FF = 4 * D_MODEL
PAGE_SIZE = 128
ROPE_BASE = 10000.0
EPS = 1e-6
LOG2_E = float(np.log2(np.e))

RET_QK_W = RET_HEADS * RET_DK
RET_V_W = RET_HEADS * RET_DV
MLA_V_W = MLA_HEADS * V_DIM
SPLITS = (RET_QK_W, RET_QK_W, RET_V_W, RET_V_W, Q_LORA, KV_LORA, ROPE_DIM, D_MODEL, D_MODEL)

LANES = 128
HEAD_PAD = LANES
MLA_PAD_W = MLA_HEADS * HEAD_PAD
_SEG_W = (RET_QK_W, RET_QK_W, RET_V_W, RET_V_W, Q_LORA, KV_LORA, LANES, D_MODEL, D_MODEL)
_SEG_ORDER = (4, 5, 6, 0, 1, 2, 3, 7, 8)
_SEG_START = dict(zip(_SEG_ORDER, np.cumsum([0] + [_SEG_W[i] for i in _SEG_ORDER[:-1]]).tolist()))
_SEG_O = tuple(_SEG_START[i] for i in range(len(_SEG_W)))
IN_PAD_W = sum(_SEG_W)
VMEM_LIMIT = 56 * 1024 * 1024

F32 = jnp.float32
BF16 = jnp.bfloat16
_NT = (((1,), (1,)), ((), ()))
_TN = (((0,), (0,)), ((), ()))


def _dot(a, b):
    return jnp.dot(a, b, preferred_element_type=F32)


def _dot_nt(a, b):
    return lax.dot_general(a, b, _NT, preferred_element_type=F32)


def _const_spec(shape):
    zeros = (0,) * len(shape)
    return pl.BlockSpec(shape, lambda *_: zeros, pipeline_mode=pl.Buffered(1))


def _params(sem):
    return pltpu.CompilerParams(dimension_semantics=sem, vmem_limit_bytes=VMEM_LIMIT)


_PROJ_IN = 15


def _proj_kernel(*refs, tiles_per_seq):
    (x_ref, cosr_ref, sinr_ref, cm_ref, sa_ref, sb_ref, nmix_ref, win_ref, gqa_ref,
     wqb_ref, gkva_ref, wukp_ref, wuvt_ref, gq_ref, gk_ref) = refs[:_PROJ_IN]
    if tiles_per_seq is None:
        rq_ref, rk_ref, rv_ref, rg_ref, gr_ref, gm_ref, q_ref, k_ref, vt_ref, c_ref, kr_ref = refs[_PROJ_IN:]
    else:
        dec_ref, qdec_ref, kdec_ref, sdec_ref, s0_ref = refs[_PROJ_IN:_PROJ_IN + 5]
        (ret_ref, gr_ref, gm_ref, q_ref, k_ref, vt_ref, c_ref, kr_ref, st_ref,
         rq_ref, rk_ref, rv_ref, rg_ref, s_scr) = refs[_PROJ_IN + 5:]
        tile = pl.program_id(0) % tiles_per_seq

        @pl.when(tile == 0)
        def _():
            s_scr[...] = s0_ref[0]

    x = x_ref[...]
    xn = x * lax.rsqrt(jnp.mean(x * x, axis=-1, keepdims=True) + EPS) * nmix_ref[...]
    xb = xn.astype(BF16)

    def seg(i):
        return _dot(xb, win_ref[:, _SEG_O[i]:_SEG_O[i] + _SEG_W[i]])

    cosr, sinr = cosr_ref[...], sinr_ref[...]
    cm, sa, sb = cm_ref[...], sa_ref[...], sb_ref[...]

    def ret_rotary(z, out_ref, scale):
        for h in range(RET_HEADS):
            zh = z[:, h * RET_DK:(h + 1) * RET_DK]
            r = zh * cosr + pltpu.roll(zh, RET_DK // 2, 1) * sinr
            if scale is not None:
                r = r * scale
            out_ref[:, h * RET_DK:(h + 1) * RET_DK] = r.astype(out_ref.dtype)

    def mla_rotary(zh):
        half = ROPE_DIM // 2
        return zh * cm + pltpu.roll(zh, HEAD_PAD - half, 1) * sa + pltpu.roll(zh, half, 1) * sb

    def head_norm(zh, g):
        ss = jnp.sum(zh * zh, axis=-1, keepdims=True)
        return zh * lax.rsqrt(ss * (1.0 / QK_DIM) + EPS) * g

    qa = seg(4)
    qn = qa * lax.rsqrt(jnp.mean(qa * qa, axis=-1, keepdims=True) + EPS) * gqa_ref[...]
    ckv = seg(5)
    c = ckv * lax.rsqrt(jnp.mean(ckv * ckv, axis=-1, keepdims=True) + EPS) * gkva_ref[...]
    c_ref[...] = c
    cb = c.astype(BF16)
    krp = mla_rotary(seg(6))
    kr_ref[...] = krp[:, NOPE_DIM:QK_DIM]

    ret_rotary(seg(0), rq_ref, None)
    ret_rotary(seg(1), rk_ref, RET_DK ** -0.5)
    rv_ref[...] = seg(2).astype(rv_ref.dtype)
    rg_ref[...] = seg(3).astype(rg_ref.dtype)

    qp = _dot(qn.astype(BF16), wqb_ref[...])
    gq = gq_ref[...]
    for h in range(MLA_HEADS):
        qh = mla_rotary(qp[:, h * HEAD_PAD:(h + 1) * HEAD_PAD])
        q_ref[:, h * HEAD_PAD:(h + 1) * HEAD_PAD] = head_norm(qh, gq).astype(q_ref.dtype)

    kp = _dot(cb, wukp_ref[...])
    gk = gk_ref[...]
    for h in range(MLA_HEADS):
        kh = kp[:, h * HEAD_PAD:(h + 1) * HEAD_PAD] + krp
        k_ref[:, h * HEAD_PAD:(h + 1) * HEAD_PAD] = head_norm(kh, gk).astype(k_ref.dtype)
    vt_w = vt_ref.shape[-1]
    for t in range(vt_ref.shape[0]):
        vt_ref[t] = _dot_nt(wuvt_ref[...], cb[t * vt_w:(t + 1) * vt_w]).astype(vt_ref.dtype)

    def branch_gates():
        gr_ref[...] = seg(7).astype(gr_ref.dtype)
        gm_ref[...] = seg(8).astype(gm_ref.dtype)

    if tiles_per_seq is None:
        branch_gates()
        return

    for ck in range(x_ref.shape[0] // RET_CHUNK):
        rows = slice(ck * RET_CHUNK, (ck + 1) * RET_CHUNK)
        for h in range(RET_HEADS):
            qh = rq_ref[rows, h * RET_DK:(h + 1) * RET_DK]
            kh = rk_ref[rows, h * RET_DK:(h + 1) * RET_DK]
            vh = rv_ref[rows, h * RET_DV:(h + 1) * RET_DV]
            scores = _dot_nt(qh, kh) * dec_ref[h]
            o = _dot(scores.astype(BF16), vh)
            s_prev = s_scr[h]
            o = o + _dot(qh, s_prev.astype(BF16)) * qdec_ref[h]
            kd = (kh.astype(F32) * kdec_ref[h]).astype(BF16)
            s_scr[h] = sdec_ref[h] * s_prev + lax.dot_general(kd, vh, _TN, preferred_element_type=F32)
            on = o * lax.rsqrt(jnp.mean(o * o, axis=-1, keepdims=True) + EPS)
            g = rg_ref[rows, h * RET_DV:(h + 1) * RET_DV].astype(F32)
            ret_ref[rows, h * RET_DV:(h + 1) * RET_DV] = (g * jax.nn.sigmoid(g) * on).astype(ret_ref.dtype)
    branch_gates()

    @pl.when(tile == tiles_per_seq - 1)
    def _():
        st_ref[0] = s_scr[...]


def _proj(x, tabs, w, tm, vt_w, mid_dtype, retention=None):
    rows = x.shape[0]
    period = tabs[0].shape[0] // tm
    row = lambda i: (i, 0)
    tab_spec = pl.BlockSpec((tm, LANES), lambda i: (i % period, 0))
    st_shape = (RET_HEADS, RET_DK, RET_DV)
    ret_widths = (RET_QK_W, RET_QK_W, RET_V_W, RET_V_W)
    widths = ((RET_V_W,) if retention else ret_widths) + (D_MODEL, D_MODEL, MLA_PAD_W, MLA_PAD_W)
    out_shape = [jax.ShapeDtypeStruct((rows, n), mid_dtype) for n in widths]
    out_shape += [jax.ShapeDtypeStruct((rows // vt_w, MLA_V_W, vt_w), mid_dtype),
                  jax.ShapeDtypeStruct((rows, KV_LORA), F32), jax.ShapeDtypeStruct((rows, ROPE_DIM), F32)]
    out_specs = [pl.BlockSpec((tm, n), row) for n in widths]
    out_specs += [pl.BlockSpec((tm // vt_w, MLA_V_W, vt_w), lambda i: (i, 0, 0)),
                  pl.BlockSpec((tm, KV_LORA), row), pl.BlockSpec((tm, ROPE_DIM), row)]
    consts = (w['norm_mix'], w['w_in'], w['g_qa'], w['w_qb'], w['g_kva'], w['w_ukp'], w['w_uv_t'],
              w['gq'], w['gk'])
    scratch, tiles_per_seq = [], None
    if retention:
        seq_rows, dtabs, s0 = retention
        tiles_per_seq = seq_rows // tm
        consts += tuple(dtabs) + (s0,)
        out_shape.append(jax.ShapeDtypeStruct((rows // seq_rows,) + st_shape, F32))
        out_specs.append(pl.BlockSpec((1,) + st_shape, lambda i: (i // tiles_per_seq, 0, 0, 0)))
        scratch = [pltpu.VMEM((tm, n), mid_dtype) for n in ret_widths] + [pltpu.VMEM(st_shape, F32)]
    return pl.pallas_call(
        functools.partial(_proj_kernel, tiles_per_seq=tiles_per_seq),
        grid=(rows // tm,),
        in_specs=[pl.BlockSpec((tm, D_MODEL), row)] + [tab_spec] * 5 + [_const_spec(a.shape) for a in consts],
        out_specs=out_specs,
        out_shape=out_shape,
        scratch_shapes=scratch,
        compiler_params=_params(("arbitrary" if retention else "parallel",)),
        name="proj",
    )(x, *tabs, *consts)


ATT_TQ = 512
ATT_TK = 256


def _attention_kernel(q_ref, k_ref, vt_ref, km_ref, vmt_ref, o_ref, qt_scr, s_scr, m_scr, l_scr, acc_scr):
    qi = pl.program_id(1)
    meta_rows = km_ref.shape[0]
    hs = lambda h: slice(h * HEAD_PAD, (h + 1) * HEAD_PAD)
    vs = lambda h: slice(h * V_DIM, (h + 1) * V_DIM)

    qt_scr[...] = q_ref[...].T
    m_scr[...] = jnp.full(m_scr.shape, -jnp.inf, F32)
    l_scr[...] = jnp.zeros(l_scr.shape, F32)
    acc_scr[...] = jnp.zeros(acc_scr.shape, F32)

    def update(n, keys, values_t, mask):
        for h in range(MLA_HEADS):
            s_scr[h, 0:n, :] = _dot(keys(h), qt_scr[hs(h), :])
        for h in range(MLA_HEADS):
            s = s_scr[h, 0:n, :]
            if mask is not None:
                s = jnp.where(mask, s, -jnp.inf)
            m_old = m_scr[h]
            m_new = jnp.maximum(m_old, jnp.max(s, axis=0, keepdims=True))
            alpha = jnp.exp2(m_old - m_new)
            p = jnp.exp2(s - m_new)
            m_scr[h] = m_new
            l_scr[h] = alpha * l_scr[h] + jnp.sum(p, axis=0, keepdims=True)
            acc_scr[vs(h), :] = alpha * acc_scr[vs(h), :] + _dot(values_t(h), p.astype(BF16))

    meta_ok = lax.broadcasted_iota(jnp.int32, (meta_rows, ATT_TQ), 0) >= meta_rows - N_META
    update(meta_rows, lambda h: km_ref[:, hs(h)], lambda h: vmt_ref[vs(h), :], meta_ok)

    def key_tile(j, mask):
        start = pl.multiple_of(j * ATT_TK, ATT_TK)
        update(ATT_TK, lambda h: k_ref[pl.ds(start, ATT_TK), hs(h)], lambda h: vt_ref[j, vs(h), :], mask)

    def body(j, carry):
        key_tile(j, None)
        return carry

    first_diag = qi * (ATT_TQ // ATT_TK)
    lax.fori_loop(0, first_diag, body, 0)
    key_pos = lax.broadcasted_iota(jnp.int32, (ATT_TK, ATT_TQ), 0)
    query_pos = lax.broadcasted_iota(jnp.int32, (ATT_TK, ATT_TQ), 1)
    for d in range(ATT_TQ // ATT_TK):
        key_tile(first_diag + d, key_pos + d * ATT_TK <= query_pos)

    for h in range(MLA_HEADS):
        acc_scr[vs(h), :] = acc_scr[vs(h), :] / l_scr[h]
    o_ref[...] = acc_scr[...].T.astype(o_ref.dtype)


def _attention(q, k, vt, km, vmt):
    nb, s = q.shape[:2]
    return pl.pallas_call(
        _attention_kernel,
        grid=(nb, s // ATT_TQ),
        in_specs=[pl.BlockSpec((None, ATT_TQ, MLA_PAD_W), lambda b, i: (b, i, 0)),
                  pl.BlockSpec((None, s, MLA_PAD_W), lambda b, i: (b, 0, 0)),
                  pl.BlockSpec((None, s // ATT_TK, MLA_V_W, ATT_TK), lambda b, i: (b, 0, 0, 0)),
                  _const_spec(km.shape), _const_spec(vmt.shape)],
        out_specs=pl.BlockSpec((None, ATT_TQ, MLA_V_W), lambda b, i: (b, i, 0)),
        out_shape=jax.ShapeDtypeStruct((nb, s, MLA_V_W), BF16),
        scratch_shapes=[pltpu.VMEM((MLA_PAD_W, ATT_TQ), BF16),
                        pltpu.VMEM((MLA_HEADS, ATT_TK, ATT_TQ), F32),
                        pltpu.VMEM((MLA_HEADS, 1, ATT_TQ), F32), pltpu.VMEM((MLA_HEADS, 1, ATT_TQ), F32),
                        pltpu.VMEM((MLA_V_W, ATT_TQ), F32)],
        compiler_params=_params(("parallel", "arbitrary")),
        name="attention",
    )(q, k, vt, km, vmt)


FF_BLOCK = 1024


def _finish_kernel(x_ref, ret_ref, mla_ref, gr_ref, gm_ref, wro_ref, wmo_ref, wout_ref, nffn_ref,
                   wup_ref, wdn_ref, y_ref):
    r = _dot(ret_ref[...].astype(BF16), wro_ref[...])
    m = _dot(mla_ref[...].astype(BF16), wmo_ref[...])
    mixed = jax.nn.sigmoid(gr_ref[...].astype(F32)) * r + jax.nn.sigmoid(gm_ref[...].astype(F32)) * m
    h = x_ref[...] + _dot(mixed.astype(BF16), wout_ref[...])
    hn = h * lax.rsqrt(jnp.mean(h * h, axis=-1, keepdims=True) + EPS) * nffn_ref[...]
    hb = hn.astype(BF16)
    y = h
    for f in range(D_FF // FF_BLOCK):
        u = jnp.maximum(_dot(hb, wup_ref[:, f * FF_BLOCK:(f + 1) * FF_BLOCK]), 0.0)
        y = y + _dot((u * u).astype(BF16), wdn_ref[f * FF_BLOCK:(f + 1) * FF_BLOCK, :])
    y_ref[...] = y


def _finish(x, ret_mid, o_mla, gr, gm, w, tm):
    rows = x.shape[0]
    row = lambda i: (i, 0)
    consts = (w['w_ret_o'], w['w_mla_o'], w['w_out'], w['norm_ffn'], w['w_up'], w['w_down'])
    return pl.pallas_call(
        _finish_kernel,
        grid=(rows // tm,),
        in_specs=[pl.BlockSpec((tm, D_MODEL), row), pl.BlockSpec((tm, RET_V_W), row),
                  pl.BlockSpec((tm, MLA_V_W), row), pl.BlockSpec((tm, D_MODEL), row),
                  pl.BlockSpec((tm, D_MODEL), row)] + [_const_spec(a.shape) for a in consts],
        out_specs=pl.BlockSpec((tm, D_MODEL), row),
        out_shape=jax.ShapeDtypeStruct((rows, D_MODEL), F32),
        compiler_params=_params(("parallel",)),
        name="finish",
    )(x, ret_mid, o_mla, gr, gm, *consts)


RET_DEC_GROUP = 8


def _column(row_vec, eye):
    return jnp.sum(jnp.where(eye, row_vec, 0.0), axis=-1, keepdims=True)


def _ret_decode_kernel(q_ref, k_ref, v_ref, rg_ref, gam_ref, s_ref, o_ref, sn_ref):
    eye = (lax.broadcasted_iota(jnp.int32, (RET_DK, RET_DK), 0)
           == lax.broadcasted_iota(jnp.int32, (RET_DK, RET_DK), 1))
    for b in range(RET_DEC_GROUP):
        for h in range(RET_HEADS):
            k_col = _column(k_ref[b:b + 1, h * RET_DK:(h + 1) * RET_DK].astype(F32), eye)
            q_col = _column(q_ref[b:b + 1, h * RET_DK:(h + 1) * RET_DK].astype(F32), eye)
            v_row = v_ref[b:b + 1, h * RET_DV:(h + 1) * RET_DV].astype(F32)
            gam = gam_ref[h]
            s_prev = s_ref[b, h]
            kv = k_col * v_row
            sn_ref[b, h] = gam * s_prev + kv
            o = jnp.sum(q_col * kv, axis=0, keepdims=True) + gam * jnp.sum(q_col * s_prev, axis=0, keepdims=True)
            on = o * lax.rsqrt(jnp.mean(o * o, axis=-1, keepdims=True) + EPS)
            g = rg_ref[b:b + 1, h * RET_DV:(h + 1) * RET_DV].astype(F32)
            o_ref[b:b + 1, h * RET_DV:(h + 1) * RET_DV] = (g * jax.nn.sigmoid(g) * on).astype(o_ref.dtype)


def _ret_decode(rq, rk, rv, rg, gam, state):
    nb = rq.shape[0]
    g = RET_DEC_GROUP
    row = lambda i: (i, 0)
    st_spec = pl.BlockSpec((g, RET_HEADS, RET_DK, RET_DV), lambda i: (i, 0, 0, 0))
    return pl.pallas_call(
        _ret_decode_kernel,
        grid=(nb // g,),
        in_specs=[pl.BlockSpec((g, RET_QK_W), row), pl.BlockSpec((g, RET_QK_W), row),
                  pl.BlockSpec((g, RET_V_W), row), pl.BlockSpec((g, RET_V_W), row),
                  _const_spec(gam.shape), st_spec],
        out_specs=[pl.BlockSpec((g, RET_V_W), row), st_spec],
        out_shape=[jax.ShapeDtypeStruct((nb, RET_V_W), F32), jax.ShapeDtypeStruct(state.shape, F32)],
        compiler_params=_params(("parallel",)),
        name="ret_decode",
    )(rq, rk, rv, rg, gam, state)


def _absorb_kernel(q_ref, k_ref, gk_ref, wukp_ref, qt_ref, qg_ref, s0_ref):
    q = q_ref[...]
    lane = lax.broadcasted_iota(jnp.int32, (q.shape[0], LANES), 1)
    s0 = jnp.zeros((q.shape[0], LANES), F32)
    gk = gk_ref[...]
    for h in range(MLA_HEADS):
        sl = slice(h * HEAD_PAD, (h + 1) * HEAD_PAD)
        qh = q[:, sl]
        s0 = s0 + jnp.where(lane == h, jnp.sum(qh * k_ref[:, sl], axis=-1, keepdims=True), 0.0)
        qg = qh * gk
        qg_ref[:, sl] = qg
        qt_ref[h] = lax.dot_general(qg, wukp_ref[:, sl], _NT, precision=lax.Precision.HIGHEST,
                                    preferred_element_type=F32)
    s0_ref[...] = s0


def _absorb(q, k, gk, wukp):
    nb = q.shape[0]
    return pl.pallas_call(
        _absorb_kernel,
        out_shape=[jax.ShapeDtypeStruct((MLA_HEADS, nb, KV_LORA), F32),
                   jax.ShapeDtypeStruct((nb, MLA_PAD_W), F32),
                   jax.ShapeDtypeStruct((nb, LANES), F32)],
        compiler_params=pltpu.CompilerParams(vmem_limit_bytes=VMEM_LIMIT),
        name="absorb",
    )(q, k, gk, wukp)


DEC_PAGES = 32
DEC_T = DEC_PAGES * PAGE_SIZE
DEC_SUB = 512
UK_ROWS = MLA_HEADS * NOPE_DIM
QT_ROWS = 16
DEC_AHEAD = 2
DEC_SLOTS = DEC_AHEAD + 1


def _mla_decode_kernel(pt_ref, lhsw_ref, qt_ref, qrope_ref, s0_ref, cnew_ref, wuv_ref, ckv_hbm, krt_hbm,
                       o_ref, lhs_scr, cbuf, kbuf, cb_scr, big_scr, sem):
    b = pl.program_id(0)
    nb = pl.num_programs(0)
    n_groups = pt_ref.shape[1] // DEC_PAGES

    def page_copies(seq, g, slot):
        copies = []
        for i in range(DEC_PAGES):
            page = pt_ref[seq, g * DEC_PAGES + i]
            tok = pl.ds(i * PAGE_SIZE, PAGE_SIZE)
            copies.append(pltpu.make_async_copy(ckv_hbm.at[0, page], cbuf.at[slot, tok], sem.at[0, slot]))
            copies.append(pltpu.make_async_copy(krt_hbm.at[0, page], kbuf.at[slot, :, tok], sem.at[1, slot]))
        return copies

    def start_group(seq, g, slot):
        for cp in page_copies(seq, g, slot):
            cp.start()

    def wait_group(slot):
        for cp in page_copies(0, 0, slot):
            cp.wait()

    def scores(slot, cb_slot):
        cb = cbuf[slot].astype(BF16)
        cb_scr[cb_slot] = cb
        big_scr[...] = _dot_nt(lhs_scr[...], cb)
        rope = _dot(qrope_ref[0], kbuf[slot].astype(BF16))
        pieces = []
        for t in range(DEC_T // DEC_SUB):
            tok = pl.ds(t * DEC_SUB, DEC_SUB)
            sq = [big_scr[d * MLA_HEADS:(d + 1) * MLA_HEADS, tok] for d in range(NOPE_DIM)]
            sq = [x * x for x in sq]
            while len(sq) > 1:
                sq = [sq[i] + sq[i + 1] for i in range(0, len(sq), 2)]
            krt = kbuf[slot, :, tok]
            kk = krt * krt
            kk = (kk[0:8] + kk[8:16]) + (kk[16:24] + kk[24:32])
            ss = sq[0] + jnp.sum(kk, axis=0, keepdims=True)
            num = big_scr[UK_ROWS:UK_ROWS + MLA_HEADS, tok] + rope[0:MLA_HEADS, t * DEC_SUB:(t + 1) * DEC_SUB]
            pieces.append(num * lax.rsqrt(ss * (1.0 / QK_DIM) + EPS))
        return jnp.concatenate(pieces, axis=-1)

    def softmax_update(s, slot, m, l, acc):
        m_new = jnp.maximum(m, jnp.max(s, axis=-1, keepdims=True))
        alpha = jnp.exp2(m - m_new)
        p = jnp.exp2(s - m_new)
        l = alpha * l + jnp.sum(p, axis=-1, keepdims=True)
        acc = alpha * acc + _dot(p.astype(BF16), cb_scr[slot])
        return m_new, l, acc

    last = nb * n_groups - 1

    def fetch(n):
        src = jnp.minimum(n, last)
        start_group(lax.div(src, n_groups), lax.rem(src, n_groups), lax.rem(n, DEC_SLOTS))

    def arrive(n):
        slot = lax.rem(n, DEC_SLOTS)
        wait_group(slot)
        return slot

    first = b * n_groups

    @pl.when(b == 0)
    def _():
        for n in range(DEC_AHEAD):
            start_group(0, n, n)

    lhs_scr[0:UK_ROWS, :] = lhsw_ref[...]
    lhs_scr[UK_ROWS:UK_ROWS + QT_ROWS, :] = qt_ref[0]

    s_first = scores(arrive(first), 0)
    fetch(first + DEC_AHEAD)

    def body(g, carry):
        s_prev, m, l, acc = carry
        s_cur = scores(arrive(first + g), g % 2)
        m, l, acc = softmax_update(s_prev, 1 - g % 2, m, l, acc)
        fetch(first + g + DEC_AHEAD)
        return s_cur, m, l, acc

    init = (s_first, s0_ref[0], jnp.ones((MLA_HEADS, 1), F32),
            jnp.broadcast_to(cnew_ref[0], (MLA_HEADS, KV_LORA)))
    s_last, m, l, acc = lax.fori_loop(1, n_groups, body, init)
    m, l, acc = softmax_update(s_last, (n_groups - 1) % 2, m, l, acc)

    @pl.when(b == nb - 1)
    def _():
        for n in range(1, DEC_AHEAD + 1):
            wait_group(lax.rem(last + n, DEC_SLOTS))

    lat = acc / l
    full = _dot(lat.astype(BF16), wuv_ref[...])
    own = (lax.broadcasted_iota(jnp.int32, full.shape, 1) // V_DIM
           == lax.broadcasted_iota(jnp.int32, full.shape, 0))
    o_ref[0] = jnp.sum(jnp.where(own, full, 0.0), axis=0, keepdims=True)


def _mla_decode(page_table, lhsw, qt, qrope, s0, c_new, wuv, cache_ckv, cache_krt):
    nb, n_pages = page_table.shape
    assert n_pages % DEC_PAGES == 0 and n_pages // DEC_PAGES >= DEC_AHEAD
    per_seq = lambda shape: pl.BlockSpec((1,) + shape, lambda i, pt: (i, 0, 0))
    const = lambda a: pl.BlockSpec(a.shape, lambda i, pt: (0,) * a.ndim, pipeline_mode=pl.Buffered(1))
    grid_spec = pltpu.PrefetchScalarGridSpec(
        num_scalar_prefetch=1,
        grid=(nb,),
        in_specs=[const(lhsw), per_seq((QT_ROWS, KV_LORA)), per_seq((QT_ROWS, ROPE_DIM)),
                  per_seq((MLA_HEADS, 1)), per_seq((1, KV_LORA)), const(wuv),
                  pl.BlockSpec(memory_space=pl.ANY), pl.BlockSpec(memory_space=pl.ANY)],
        out_specs=per_seq((1, MLA_V_W)),
        scratch_shapes=[pltpu.VMEM((UK_ROWS + QT_ROWS, KV_LORA), BF16),
                        pltpu.VMEM((DEC_SLOTS, DEC_T, KV_LORA), F32),
                        pltpu.VMEM((DEC_SLOTS, ROPE_DIM, DEC_T), F32),
                        pltpu.VMEM((2, DEC_T, KV_LORA), BF16),
                        pltpu.VMEM((UK_ROWS + QT_ROWS, DEC_T), F32),
                        pltpu.SemaphoreType.DMA((2, DEC_SLOTS))],
    )
    return pl.pallas_call(
        _mla_decode_kernel,
        grid_spec=grid_spec,
        out_shape=jax.ShapeDtypeStruct((nb, 1, MLA_V_W), F32),
        compiler_params=_params(("arbitrary",)),
        name="mla_decode",
    )(page_table, lhsw, qt, qrope, s0, c_new, wuv, cache_ckv, cache_krt)


def _rotary_tables(pos):
    pos = pos.astype(F32)[:, None]

    def cos_sin(half):
        inv = ROPE_BASE ** (-jnp.arange(half, dtype=F32) / half)
        ang = pos * inv[None, :]
        return jnp.cos(ang), jnp.sin(ang)

    cr, sr = cos_sin(RET_DK // 2)
    cosr = jnp.concatenate([cr, cr], axis=-1)
    sinr = jnp.concatenate([-sr, sr], axis=-1)
    c, s = cos_sin(ROPE_DIM // 2)
    n = pos.shape[0]
    half = ROPE_DIM // 2
    z = lambda w: jnp.zeros((n, w), F32)
    cm = jnp.concatenate([jnp.ones((n, NOPE_DIM), F32), c, c, z(HEAD_PAD - QK_DIM)], axis=-1)
    sa = jnp.concatenate([z(NOPE_DIM), -s, z(half), z(HEAD_PAD - QK_DIM)], axis=-1)
    sb = jnp.concatenate([z(NOPE_DIM), z(half), s, z(HEAD_PAD - QK_DIM)], axis=-1)
    return cosr, sinr, cm, sa, sb


def _decay_tables(c):
    lg = jnp.log1p(-jnp.exp2(-5.0 - jnp.arange(RET_HEADS, dtype=F32)))
    idx = jnp.arange(c, dtype=F32)
    diff = idx[:, None] - idx[None, :]
    dec = jnp.where(diff[None] >= 0, jnp.exp(jnp.maximum(diff, 0.0)[None] * lg[:, None, None]), 0.0)
    qdec = jnp.exp((idx + 1.0)[None, :] * lg[:, None])[:, :, None]
    kdec = jnp.exp((c - 1.0 - idx)[None, :] * lg[:, None])[:, :, None]
    sdec = jnp.exp(c * lg)[:, None, None]
    return dec, qdec, kdec, sdec


def _pad_heads(a):
    a = jnp.pad(a, [(0, 0)] * (a.ndim - 1) + [(0, HEAD_PAD - a.shape[-1])])
    return a.reshape(a.shape[:-2] + (a.shape[-2] * HEAD_PAD,))


def _layer_weights(norm_mix, w_in, g_qa, w_qb, g_kva, w_uk, w_uv, g_qn, g_kn, w_ret_o, w_mla_o, w_out,
                   norm_ffn, w_up, w_down):
    idx = [int(i) for i in np.cumsum(SPLITS)[:-1]]
    parts = jnp.split(w_in, idx, axis=-1)
    parts[6] = jnp.pad(parts[6], ((0, 0), (NOPE_DIM, HEAD_PAD - QK_DIM)))
    pad1 = lambda g: jnp.pad(g, (0, HEAD_PAD - QK_DIM))[None, :]
    wukp = _pad_heads(w_uk)
    return {
        'norm_mix': norm_mix[None, :],
        'w_in': jnp.concatenate([parts[i] for i in _SEG_ORDER], axis=-1).astype(BF16),
        'g_qa': g_qa[None, :],
        'w_qb': _pad_heads(w_qb.reshape(Q_LORA, MLA_HEADS, QK_DIM)).astype(BF16),
        'g_kva': g_kva[None, :],
        'w_ukp': wukp.astype(BF16),
        'w_ukp_f32': wukp,
        'w_uk_t': w_uk.transpose(2, 1, 0).reshape(UK_ROWS, KV_LORA).astype(BF16),
        'w_uv': w_uv.reshape(KV_LORA, MLA_V_W).astype(BF16),
        'w_uv_t': w_uv.reshape(KV_LORA, MLA_V_W).T.astype(BF16),
        'gq': pad1(g_qn) * (QK_DIM ** -0.5 * LOG2_E),
        'gk': pad1(g_kn),
        'w_ret_o': w_ret_o.astype(BF16),
        'w_mla_o': w_mla_o.astype(BF16),
        'w_out': w_out.astype(BF16),
        'norm_ffn': norm_ffn[None, :],
        'w_up': w_up.astype(BF16),
        'w_down': w_down.astype(BF16),
    }


PROJ_TM = 256
FINISH_TM = 512


def kernel(x_prompt, x_sample, cache_ckv, cache_krope, state_ret, page_table, meta_tokens, norm_mix, w_in,
           g_qa, w_qb, g_kva, w_uk, w_uv, g_qn, g_kn, w_ret_o, w_mla_o, w_out, norm_ffn, w_up, w_down):
    nb, seq, _ = x_prompt.shape
    ns = x_sample.shape[0]
    assert x_sample.shape[1] == 1 and norm_mix.shape[0] == 1
    past = page_table.shape[1] * PAGE_SIZE
    w = _layer_weights(norm_mix[0], w_in[0], g_qa[0], w_qb[0], g_kva[0], w_uk[0], w_uv[0], g_qn[0], g_kn[0],
                       w_ret_o[0], w_mla_o[0], w_out[0], norm_ffn[0], w_up[0], w_down[0])
    dtabs = _decay_tables(RET_CHUNK)

    mpad = RET_CHUNK - N_META
    xm = jnp.pad(meta_tokens, ((mpad, 0), (0, 0)))
    pos_m = jnp.maximum(jnp.arange(RET_CHUNK) - mpad, 0)
    zero_state = jnp.zeros((1, RET_HEADS, RET_DK, RET_DV), F32)
    _, _, _, _, km, vmt, c_m, kr_m, s0 = _proj(xm, _rotary_tables(pos_m), w, RET_CHUNK, RET_CHUNK, BF16,
                                               retention=(RET_CHUNK, dtabs, zero_state))

    xp = x_prompt.reshape(nb * seq, D_MODEL)
    ret_mid, gr, gm, q, k, vt, c_p, kr_p, st_p = _proj(
        xp, _rotary_tables(N_META + jnp.arange(seq)), w, PROJ_TM, ATT_TK, BF16, retention=(seq, dtabs, s0))
    b3 = lambda a: a.reshape(nb, seq, a.shape[-1])
    o_mla = _attention(b3(q), b3(k), vt.reshape(nb, seq // ATT_TK, MLA_V_W, ATT_TK), km, vmt[0])
    y_prompt = _finish(xp, ret_mid, o_mla.reshape(nb * seq, MLA_V_W), gr, gm, w,
                       FINISH_TM).reshape(nb, seq, D_MODEL)
    bcast = lambda a: jnp.broadcast_to(a[None, mpad:], (nb, N_META, a.shape[-1]))
    ckv_prompt = jnp.concatenate([bcast(c_m), b3(c_p)], axis=1)[None]
    krope_prompt = jnp.concatenate([bcast(kr_m), b3(kr_p)], axis=1)[None]

    xs = x_sample.reshape(ns, D_MODEL)
    pos_s = jnp.full((ns,), past, jnp.int32)
    rq, rk, rv, rg, gr, gm, q, k, _, c_s, kr_s = _proj(xs, _rotary_tables(pos_s), w, ns, ns, F32)
    gam = jnp.exp(jnp.log1p(-jnp.exp2(-5.0 - jnp.arange(RET_HEADS, dtype=F32))))[:, None, None]
    ret_mid, st_s = _ret_decode(rq, rk, rv, rg, gam, state_ret[0])
    qt, qg, s0_new = _absorb(q, k, w['gk'], w['w_ukp_f32'])
    qt = jnp.pad(qt.transpose(1, 0, 2), ((0, 0), (0, QT_ROWS - MLA_HEADS), (0, 0))).astype(BF16)
    q_rope = qg.reshape(ns, MLA_HEADS, HEAD_PAD)[:, :, NOPE_DIM:QK_DIM]
    q_rope = jnp.pad(q_rope, ((0, 0), (0, QT_ROWS - MLA_HEADS), (0, 0))).astype(BF16)
    o_mla = _mla_decode(page_table, w['w_uk_t'], qt, q_rope, s0_new[:, :MLA_HEADS, None], c_s[:, None, :],
                        w['w_uv'], cache_ckv, jnp.swapaxes(cache_krope, 2, 3))
    y_sample = _finish(xs, ret_mid, o_mla.reshape(ns, MLA_V_W), gr, gm, w, ns).reshape(ns, 1, D_MODEL)

    return (y_prompt, y_sample, ckv_prompt, krope_prompt, st_p[None],
            c_s.reshape(1, ns, 1, KV_LORA), kr_s.reshape(1, ns, 1, ROPE_DIM), st_s[None])
```

```python
import functools

import numpy as np
import jax
import jax.numpy as jnp
from jax import lax
from jax.experimental import pallas as pl
from jax.experimental.pallas import tpu as pltpu

D_MODEL = 1024
N_META = 16
RET_HEADS = 4
RET_DK = 128
RET_DV = 256
RET_CHUNK = 128
MLA_HEADS = 8
Q_LORA = 384
KV_LORA = 256
NOPE_DIM = 64
ROPE_DIM = 32
QK_DIM = NOPE_DIM + ROPE_DIM
V_DIM = 64
D_FF = 4 * D_MODEL
PAGE_SIZE = 128
ROPE_BASE = 10000.0
EPS = 1e-6
LOG2_E = float(np.log2(np.e))

RET_QK_W = RET_HEADS * RET_DK
RET_V_W = RET_HEADS * RET_DV
MLA_V_W = MLA_HEADS * V_DIM
SPLITS = (RET_QK_W, RET_QK_W, RET_V_W, RET_V_W, Q_LORA, KV_LORA, ROPE_DIM, D_MODEL, D_MODEL)

LANES = 128
HEAD_PAD = LANES
MLA_PAD_W = MLA_HEADS * HEAD_PAD
_SEG_W = (RET_QK_W, RET_QK_W, RET_V_W, RET_V_W, Q_LORA, KV_LORA, LANES, D_MODEL, D_MODEL)
_SEG_ORDER = (4, 5, 6, 0, 1, 2, 3, 7, 8)
_SEG_START = dict(zip(_SEG_ORDER, np.cumsum([0] + [_SEG_W[i] for i in _SEG_ORDER[:-1]]).tolist()))
_SEG_O = tuple(_SEG_START[i] for i in range(len(_SEG_W)))
IN_PAD_W = sum(_SEG_W)
VMEM_LIMIT = 56 * 1024 * 1024

F32 = jnp.float32
BF16 = jnp.bfloat16
_NT = (((1,), (1,)), ((), ()))
_TN = (((0,), (0,)), ((), ()))


def _dot(a, b):
    return jnp.dot(a, b, preferred_element_type=F32)


def _dot_nt(a, b):
    return lax.dot_general(a, b, _NT, preferred_element_type=F32)


def _const_spec(shape):
    zeros = (0,) * len(shape)
    return pl.BlockSpec(shape, lambda *_: zeros, pipeline_mode=pl.Buffered(1))


def _params(sem):
    return pltpu.CompilerParams(dimension_semantics=sem, vmem_limit_bytes=VMEM_LIMIT)


_PROJ_IN = 15


def _proj_kernel(*refs, tiles_per_seq):
    (x_ref, cosr_ref, sinr_ref, cm_ref, sa_ref, sb_ref, nmix_ref, win_ref, gqa_ref,
     wqb_ref, gkva_ref, wukp_ref, wuvt_ref, gq_ref, gk_ref) = refs[:_PROJ_IN]
    if tiles_per_seq is None:
        rq_ref, rk_ref, rv_ref, rg_ref, gr_ref, gm_ref, q_ref, k_ref, vt_ref, c_ref, kr_ref = refs[_PROJ_IN:]
    else:
        dec_ref, qdec_ref, kdec_ref, sdec_ref, s0_ref = refs[_PROJ_IN:_PROJ_IN + 5]
        (ret_ref, gr_ref, gm_ref, q_ref, k_ref, vt_ref, c_ref, kr_ref, st_ref,
         rq_ref, rk_ref, rv_ref, rg_ref, s_scr) = refs[_PROJ_IN + 5:]
        tile = pl.program_id(0) % tiles_per_seq

        @pl.when(tile == 0)
        def _():
            s_scr[...] = s0_ref[0]

    x = x_ref[...]
    xn = x * lax.rsqrt(jnp.mean(x * x, axis=-1, keepdims=True) + EPS) * nmix_ref[...]
    xb = xn.astype(BF16)

    def seg(i):
        return _dot(xb, win_ref[:, _SEG_O[i]:_SEG_O[i] + _SEG_W[i]])

    cosr, sinr = cosr_ref[...], sinr_ref[...]
    cm, sa, sb = cm_ref[...], sa_ref[...], sb_ref[...]

    def ret_rotary(z, out_ref, scale):
        for h in range(RET_HEADS):
            zh = z[:, h * RET_DK:(h + 1) * RET_DK]
            r = zh * cosr + pltpu.roll(zh, RET_DK // 2, 1) * sinr
            if scale is not None:
                r = r * scale
            out_ref[:, h * RET_DK:(h + 1) * RET_DK] = r.astype(out_ref.dtype)

    def mla_rotary(zh):
        half = ROPE_DIM // 2
        return zh * cm + pltpu.roll(zh, HEAD_PAD - half, 1) * sa + pltpu.roll(zh, half, 1) * sb

    def head_norm(zh, g):
        ss = jnp.sum(zh * zh, axis=-1, keepdims=True)
        return zh * lax.rsqrt(ss * (1.0 / QK_DIM) + EPS) * g

    qa = seg(4)
    qn = qa * lax.rsqrt(jnp.mean(qa * qa, axis=-1, keepdims=True) + EPS) * gqa_ref[...]
    ckv = seg(5)
    c = ckv * lax.rsqrt(jnp.mean(ckv * ckv, axis=-1, keepdims=True) + EPS) * gkva_ref[...]
    c_ref[...] = c
    cb = c.astype(BF16)
    krp = mla_rotary(seg(6))
    kr_ref[...] = krp[:, NOPE_DIM:QK_DIM]

    ret_rotary(seg(0), rq_ref, None)
    ret_rotary(seg(1), rk_ref, RET_DK ** -0.5)
    rv_ref[...] = seg(2).astype(rv_ref.dtype)
    rg_ref[...] = seg(3).astype(rg_ref.dtype)

    qp = _dot(qn.astype(BF16), wqb_ref[...])
    gq = gq_ref[...]
    for h in range(MLA_HEADS):
        qh = mla_rotary(qp[:, h * HEAD_PAD:(h + 1) * HEAD_PAD])
        q_ref[:, h * HEAD_PAD:(h + 1) * HEAD_PAD] = head_norm(qh, gq).astype(q_ref.dtype)

    kp = _dot(cb, wukp_ref[...])
    gk = gk_ref[...]
    for h in range(MLA_HEADS):
        kh = kp[:, h * HEAD_PAD:(h + 1) * HEAD_PAD] + krp
        k_ref[:, h * HEAD_PAD:(h + 1) * HEAD_PAD] = head_norm(kh, gk).astype(k_ref.dtype)
    vt_w = vt_ref.shape[-1]
    for t in range(vt_ref.shape[0]):
        vt_ref[t] = _dot_nt(wuvt_ref[...], cb[t * vt_w:(t + 1) * vt_w]).astype(vt_ref.dtype)

    def branch_gates():
        gr_ref[...] = seg(7).astype(gr_ref.dtype)
        gm_ref[...] = seg(8).astype(gm_ref.dtype)

    if tiles_per_seq is None:
        branch_gates()
        return

    for ck in range(x_ref.shape[0] // RET_CHUNK):
        rows = slice(ck * RET_CHUNK, (ck + 1) * RET_CHUNK)
        for h in range(RET_HEADS):
            qh = rq_ref[rows, h * RET_DK:(h + 1) * RET_DK]
            kh = rk_ref[rows, h * RET_DK:(h + 1) * RET_DK]
            vh = rv_ref[rows, h * RET_DV:(h + 1) * RET_DV]
            scores = _dot_nt(qh, kh) * dec_ref[h]
            o = _dot(scores.astype(BF16), vh)
            s_prev = s_scr[h]
            o = o + _dot(qh, s_prev.astype(BF16)) * qdec_ref[h]
            kd = (kh.astype(F32) * kdec_ref[h]).astype(BF16)
            s_scr[h] = sdec_ref[h] * s_prev + lax.dot_general(kd, vh, _TN, preferred_element_type=F32)
            on = o * lax.rsqrt(jnp.mean(o * o, axis=-1, keepdims=True) + EPS)
            g = rg_ref[rows, h * RET_DV:(h + 1) * RET_DV].astype(F32)
            ret_ref[rows, h * RET_DV:(h + 1) * RET_DV] = (g * jax.nn.sigmoid(g) * on).astype(ret_ref.dtype)
    branch_gates()

    @pl.when(tile == tiles_per_seq - 1)
    def _():
        st_ref[0] = s_scr[...]


def _proj(x, tabs, w, tm, vt_w, mid_dtype, retention=None):
    rows = x.shape[0]
    period = tabs[0].shape[0] // tm
    row = lambda i: (i, 0)
    tab_spec = pl.BlockSpec((tm, LANES), lambda i: (i % period, 0))
    st_shape = (RET_HEADS, RET_DK, RET_DV)
    ret_widths = (RET_QK_W, RET_QK_W, RET_V_W, RET_V_W)
    widths = ((RET_V_W,) if retention else ret_widths) + (D_MODEL, D_MODEL, MLA_PAD_W, MLA_PAD_W)
    out_shape = [jax.ShapeDtypeStruct((rows, n), mid_dtype) for n in widths]
    out_shape += [jax.ShapeDtypeStruct((rows // vt_w, MLA_V_W, vt_w), mid_dtype),
                  jax.ShapeDtypeStruct((rows, KV_LORA), F32), jax.ShapeDtypeStruct((rows, ROPE_DIM), F32)]
    out_specs = [pl.BlockSpec((tm, n), row) for n in widths]
    out_specs += [pl.BlockSpec((tm // vt_w, MLA_V_W, vt_w), lambda i: (i, 0, 0)),
                  pl.BlockSpec((tm, KV_LORA), row), pl.BlockSpec((tm, ROPE_DIM), row)]
    consts = (w['norm_mix'], w['w_in'], w['g_qa'], w['w_qb'], w['g_kva'], w['w_ukp'], w['w_uv_t'],
              w['gq'], w['gk'])
    scratch, tiles_per_seq = [], None
    if retention:
        seq_rows, dtabs, s0 = retention
        tiles_per_seq = seq_rows // tm
        consts += tuple(dtabs) + (s0,)
        out_shape.append(jax.ShapeDtypeStruct((rows // seq_rows,) + st_shape, F32))
        out_specs.append(pl.BlockSpec((1,) + st_shape, lambda i: (i // tiles_per_seq, 0, 0, 0)))
        scratch = [pltpu.VMEM((tm, n), mid_dtype) for n in ret_widths] + [pltpu.VMEM(st_shape, F32)]
    return pl.pallas_call(
        functools.partial(_proj_kernel, tiles_per_seq=tiles_per_seq),
        grid=(rows // tm,),
        in_specs=[pl.BlockSpec((tm, D_MODEL), row)] + [tab_spec] * 5 + [_const_spec(a.shape) for a in consts],
        out_specs=out_specs,
        out_shape=out_shape,
        scratch_shapes=scratch,
        compiler_params=_params(("arbitrary" if retention else "parallel",)),
        name="proj",
    )(x, *tabs, *consts)


ATT_TQ = 512
ATT_TK = 256


def _attention_kernel(q_ref, k_ref, vt_ref, km_ref, vmt_ref, o_ref, qt_scr, s_scr, m_scr, l_scr, acc_scr):
    qi = pl.program_id(1)
    meta_rows = km_ref.shape[0]
    hs = lambda h: slice(h * HEAD_PAD, (h + 1) * HEAD_PAD)
    vs = lambda h: slice(h * V_DIM, (h + 1) * V_DIM)

    qt_scr[...] = q_ref[...].T
    m_scr[...] = jnp.full(m_scr.shape, -jnp.inf, F32)
    l_scr[...] = jnp.zeros(l_scr.shape, F32)
    acc_scr[...] = jnp.zeros(acc_scr.shape, F32)

    def update(n, keys, values_t, mask):
        for h in range(MLA_HEADS):
            s_scr[h, 0:n, :] = _dot(keys(h), qt_scr[hs(h), :])
        for h in range(MLA_HEADS):
            s = s_scr[h, 0:n, :]
            if mask is not None:
                s = jnp.where(mask, s, -jnp.inf)
            m_old = m_scr[h]
            m_new = jnp.maximum(m_old, jnp.max(s, axis=0, keepdims=True))
            alpha = jnp.exp2(m_old - m_new)
            p = jnp.exp2(s - m_new)
            m_scr[h] = m_new
            l_scr[h] = alpha * l_scr[h] + jnp.sum(p, axis=0, keepdims=True)
            acc_scr[vs(h), :] = alpha * acc_scr[vs(h), :] + _dot(values_t(h), p.astype(BF16))

    meta_ok = lax.broadcasted_iota(jnp.int32, (meta_rows, ATT_TQ), 0) >= meta_rows - N_META
    update(meta_rows, lambda h: km_ref[:, hs(h)], lambda h: vmt_ref[vs(h), :], meta_ok)

    def key_tile(j, mask):
        start = pl.multiple_of(j * ATT_TK, ATT_TK)
        update(ATT_TK, lambda h: k_ref[pl.ds(start, ATT_TK), hs(h)], lambda h: vt_ref[j, vs(h), :], mask)

    def body(j, carry):
        key_tile(j, None)
        return carry

    first_diag = qi * (ATT_TQ // ATT_TK)
    lax.fori_loop(0, first_diag, body, 0)
    key_pos = lax.broadcasted_iota(jnp.int32, (ATT_TK, ATT_TQ), 0)
    query_pos = lax.broadcasted_iota(jnp.int32, (ATT_TK, ATT_TQ), 1)
    for d in range(ATT_TQ // ATT_TK):
        key_tile(first_diag + d, key_pos + d * ATT_TK <= query_pos)

    for h in range(MLA_HEADS):
        acc_scr[vs(h), :] = acc_scr[vs(h), :] / l_scr[h]
    o_ref[...] = acc_scr[...].T.astype(o_ref.dtype)


def _attention(q, k, vt, km, vmt):
    nb, s = q.shape[:2]
    return pl.pallas_call(
        _attention_kernel,
        grid=(nb, s // ATT_TQ),
        in_specs=[pl.BlockSpec((None, ATT_TQ, MLA_PAD_W), lambda b, i: (b, i, 0)),
                  pl.BlockSpec((None, s, MLA_PAD_W), lambda b, i: (b, 0, 0)),
                  pl.BlockSpec((None, s // ATT_TK, MLA_V_W, ATT_TK), lambda b, i: (b, 0, 0, 0)),
                  _const_spec(km.shape), _const_spec(vmt.shape)],
        out_specs=pl.BlockSpec((None, ATT_TQ, MLA_V_W), lambda b, i: (b, i, 0)),
        out_shape=jax.ShapeDtypeStruct((nb, s, MLA_V_W), BF16),
        scratch_shapes=[pltpu.VMEM((MLA_PAD_W, ATT_TQ), BF16),
                        pltpu.VMEM((MLA_HEADS, ATT_TK, ATT_TQ), F32),
                        pltpu.VMEM((MLA_HEADS, 1, ATT_TQ), F32), pltpu.VMEM((MLA_HEADS, 1, ATT_TQ), F32),
                        pltpu.VMEM((MLA_V_W, ATT_TQ), F32)],
        compiler_params=_params(("parallel", "arbitrary")),
        name="attention",
    )(q, k, vt, km, vmt)


FF_BLOCK = 1024


def _finish_kernel(x_ref, ret_ref, mla_ref, gr_ref, gm_ref, wro_ref, wmo_ref, wout_ref, nffn_ref,
                   wup_ref, wdn_ref, y_ref):
    r = _dot(ret_ref[...].astype(BF16), wro_ref[...])
    m = _dot(mla_ref[...].astype(BF16), wmo_ref[...])
    mixed = jax.nn.sigmoid(gr_ref[...].astype(F32)) * r + jax.nn.sigmoid(gm_ref[...].astype(F32)) * m
    h = x_ref[...] + _dot(mixed.astype(BF16), wout_ref[...])
    hn = h * lax.rsqrt(jnp.mean(h * h, axis=-1, keepdims=True) + EPS) * nffn_ref[...]
    hb = hn.astype(BF16)
    y = h
    for f in range(D_FF // FF_BLOCK):
        u = jnp.maximum(_dot(hb, wup_ref[:, f * FF_BLOCK:(f + 1) * FF_BLOCK]), 0.0)
        y = y + _dot((u * u).astype(BF16), wdn_ref[f * FF_BLOCK:(f + 1) * FF_BLOCK, :])
    y_ref[...] = y


def _finish(x, ret_mid, o_mla, gr, gm, w, tm):
    rows = x.shape[0]
    row = lambda i: (i, 0)
    consts = (w['w_ret_o'], w['w_mla_o'], w['w_out'], w['norm_ffn'], w['w_up'], w['w_down'])
    return pl.pallas_call(
        _finish_kernel,
        grid=(rows // tm,),
        in_specs=[pl.BlockSpec((tm, D_MODEL), row), pl.BlockSpec((tm, RET_V_W), row),
                  pl.BlockSpec((tm, MLA_V_W), row), pl.BlockSpec((tm, D_MODEL), row),
                  pl.BlockSpec((tm, D_MODEL), row)] + [_const_spec(a.shape) for a in consts],
        out_specs=pl.BlockSpec((tm, D_MODEL), row),
        out_shape=jax.ShapeDtypeStruct((rows, D_MODEL), F32),
        compiler_params=_params(("parallel",)),
        name="finish",
    )(x, ret_mid, o_mla, gr, gm, *consts)


RET_DEC_GROUP = 8


def _column(row_vec, eye):
    return jnp.sum(jnp.where(eye, row_vec, 0.0), axis=-1, keepdims=True)


def _ret_decode_kernel(q_ref, k_ref, v_ref, rg_ref, gam_ref, s_ref, o_ref, sn_ref):
    eye = (lax.broadcasted_iota(jnp.int32, (RET_DK, RET_DK), 0)
           == lax.broadcasted_iota(jnp.int32, (RET_DK, RET_DK), 1))
    for b in range(RET_DEC_GROUP):
        for h in range(RET_HEADS):
            k_col = _column(k_ref[b:b + 1, h * RET_DK:(h + 1) * RET_DK].astype(F32), eye)
            q_col = _column(q_ref[b:b + 1, h * RET_DK:(h + 1) * RET_DK].astype(F32), eye)
            v_row = v_ref[b:b + 1, h * RET_DV:(h + 1) * RET_DV].astype(F32)
            gam = gam_ref[h]
            s_prev = s_ref[b, h]
            kv = k_col * v_row
            sn_ref[b, h] = gam * s_prev + kv
            o = jnp.sum(q_col * kv, axis=0, keepdims=True) + gam * jnp.sum(q_col * s_prev, axis=0, keepdims=True)
            on = o * lax.rsqrt(jnp.mean(o * o, axis=-1, keepdims=True) + EPS)
            g = rg_ref[b:b + 1, h * RET_DV:(h + 1) * RET_DV].astype(F32)
            o_ref[b:b + 1, h * RET_DV:(h + 1) * RET_DV] = (g * jax.nn.sigmoid(g) * on).astype(o_ref.dtype)


def _ret_decode(rq, rk, rv, rg, gam, state):
    nb = rq.shape[0]
    g = RET_DEC_GROUP
    row = lambda i: (i, 0)
    st_spec = pl.BlockSpec((g, RET_HEADS, RET_DK, RET_DV), lambda i: (i, 0, 0, 0))
    return pl.pallas_call(
        _ret_decode_kernel,
        grid=(nb // g,),
        in_specs=[pl.BlockSpec((g, RET_QK_W), row), pl.BlockSpec((g, RET_QK_W), row),
                  pl.BlockSpec((g, RET_V_W), row), pl.BlockSpec((g, RET_V_W), row),
                  _const_spec(gam.shape), st_spec],
        out_specs=[pl.BlockSpec((g, RET_V_W), row), st_spec],
        out_shape=[jax.ShapeDtypeStruct((nb, RET_V_W), F32), jax.ShapeDtypeStruct(state.shape, F32)],
        compiler_params=_params(("parallel",)),
        name="ret_decode",
    )(rq, rk, rv, rg, gam, state)


def _absorb_kernel(q_ref, k_ref, gk_ref, wukp_ref, qt_ref, qg_ref, s0_ref):
    q = q_ref[...]
    lane = lax.broadcasted_iota(jnp.int32, (q.shape[0], LANES), 1)
    s0 = jnp.zeros((q.shape[0], LANES), F32)
    gk = gk_ref[...]
    for h in range(MLA_HEADS):
        sl = slice(h * HEAD_PAD, (h + 1) * HEAD_PAD)
        qh = q[:, sl]
        s0 = s0 + jnp.where(lane == h, jnp.sum(qh * k_ref[:, sl], axis=-1, keepdims=True), 0.0)
        qg = qh * gk
        qg_ref[:, sl] = qg
        qt_ref[h] = lax.dot_general(qg, wukp_ref[:, sl], _NT, precision=lax.Precision.HIGHEST,
                                    preferred_element_type=F32)
    s0_ref[...] = s0


def _absorb(q, k, gk, wukp):
    nb = q.shape[0]
    return pl.pallas_call(
        _absorb_kernel,
        out_shape=[jax.ShapeDtypeStruct((MLA_HEADS, nb, KV_LORA), F32),
                   jax.ShapeDtypeStruct((nb, MLA_PAD_W), F32),
                   jax.ShapeDtypeStruct((nb, LANES), F32)],
        compiler_params=pltpu.CompilerParams(vmem_limit_bytes=VMEM_LIMIT),
        name="absorb",
    )(q, k, gk, wukp)


DEC_PAGES = 32
DEC_T = DEC_PAGES * PAGE_SIZE
DEC_SUB = 512
UK_ROWS = MLA_HEADS * NOPE_DIM
QT_ROWS = 16
DEC_AHEAD = 2
DEC_SLOTS = DEC_AHEAD + 1


def _mla_decode_kernel(pt_ref, lhsw_ref, qt_ref, qrope_ref, s0_ref, cnew_ref, wuv_ref, ckv_hbm, krt_hbm,
                       o_ref, lhs_scr, cbuf, kbuf, cb_scr, big_scr, s_st, m_st, l_st, acc_st, sem):
    b = pl.program_id(0)
    nb = pl.num_programs(0) - 1
    n_groups = pt_ref.shape[1] // DEC_PAGES

    def page_copies(seq, g, slot):
        copies = []
        for i in range(DEC_PAGES):
            page = pt_ref[seq, g * DEC_PAGES + i]
            tok = pl.ds(i * PAGE_SIZE, PAGE_SIZE)
            copies.append(pltpu.make_async_copy(ckv_hbm.at[0, page], cbuf.at[slot, tok], sem.at[0, slot]))
            copies.append(pltpu.make_async_copy(krt_hbm.at[0, page], kbuf.at[slot, :, tok], sem.at[1, slot]))
        return copies

    def start_group(seq, g, slot):
        for cp in page_copies(seq, g, slot):
            cp.start()

    def wait_group(slot):
        for cp in page_copies(0, 0, slot):
            cp.wait()

    def scores(slot, cb_slot):
        cb = cbuf[slot].astype(BF16)
        cb_scr[cb_slot] = cb
        big_scr[...] = _dot_nt(lhs_scr[...], cb)
        rope = _dot(qrope_ref[0], kbuf[slot].astype(BF16))
        pieces = []
        for t in range(DEC_T // DEC_SUB):
            tok = pl.ds(t * DEC_SUB, DEC_SUB)
            sq = [big_scr[d * MLA_HEADS:(d + 1) * MLA_HEADS, tok] for d in range(NOPE_DIM)]
            sq = [x * x for x in sq]
            while len(sq) > 1:
                sq = [sq[i] + sq[i + 1] for i in range(0, len(sq), 2)]
            krt = kbuf[slot, :, tok]
            kk = krt * krt
            kk = (kk[0:8] + kk[8:16]) + (kk[16:24] + kk[24:32])
            ss = sq[0] + jnp.sum(kk, axis=0, keepdims=True)
            num = big_scr[UK_ROWS:UK_ROWS + MLA_HEADS, tok] + rope[0:MLA_HEADS, t * DEC_SUB:(t + 1) * DEC_SUB]
            pieces.append(num * lax.rsqrt(ss * (1.0 / QK_DIM) + EPS))
        return jnp.concatenate(pieces, axis=-1)

    def softmax_update(s, slot, m, l, acc):
        m_new = jnp.maximum(m, jnp.max(s, axis=-1, keepdims=True))
        alpha = jnp.exp2(m - m_new)
        p = jnp.exp2(s - m_new)
        l = alpha * l + jnp.sum(p, axis=-1, keepdims=True)
        acc = alpha * acc + _dot(p.astype(BF16), cb_scr[slot])
        return m_new, l, acc

    last = nb * n_groups - 1

    def fetch(n):
        src = jnp.minimum(n, last)
        start_group(lax.div(src, n_groups), lax.rem(src, n_groups), lax.rem(n, DEC_SLOTS))

    def arrive(n):
        slot = lax.rem(n, DEC_SLOTS)
        wait_group(slot)
        return slot

    first = b * n_groups
    last_cb = (n_groups - 1) % 2

    def finish_previous():
        _, l, acc = softmax_update(s_st[...], last_cb, m_st[...], l_st[...], acc_st[...])
        lat = acc / l
        full = _dot(lat.astype(BF16), wuv_ref[...])
        own = (lax.broadcasted_iota(jnp.int32, full.shape, 1) // V_DIM
               == lax.broadcasted_iota(jnp.int32, full.shape, 0))
        o_ref[0] = jnp.sum(jnp.where(own, full, 0.0), axis=0, keepdims=True)

    @pl.when(b == 0)
    def _():
        for n in range(DEC_AHEAD):
            start_group(0, n, n)
        s_st[...] = jnp.zeros(s_st.shape, F32)
        m_st[...] = jnp.zeros(m_st.shape, F32)
        l_st[...] = jnp.ones(l_st.shape, F32)
        acc_st[...] = jnp.zeros(acc_st.shape, F32)
        cb_scr[last_cb] = jnp.zeros(cb_scr.shape[1:], BF16)

    @pl.when(b < nb)
    def _():
        lhs_scr[0:UK_ROWS, :] = lhsw_ref[...]
        lhs_scr[UK_ROWS:UK_ROWS + QT_ROWS, :] = qt_ref[0]

        s_first = scores(arrive(first), 0)
        fetch(first + DEC_AHEAD)
        finish_previous()

        def body(g, carry):
            s_prev, m, l, acc = carry
            s_cur = scores(arrive(first + g), g % 2)
            m, l, acc = softmax_update(s_prev, 1 - g % 2, m, l, acc)
            fetch(first + g + DEC_AHEAD)
            return s_cur, m, l, acc

        init = (s_first, s0_ref[0], jnp.ones((MLA_HEADS, 1), F32),
                jnp.broadcast_to(cnew_ref[0], (MLA_HEADS, KV_LORA)))
        s_st[...], m_st[...], l_st[...], acc_st[...] = lax.fori_loop(1, n_groups, body, init)

    @pl.when(b == nb)
    def _():
        finish_previous()
        for n in range(1, DEC_AHEAD + 1):
            wait_group(lax.rem(last + n, DEC_SLOTS))


def _mla_decode(page_table, lhsw, qt, qrope, s0, c_new, wuv, cache_ckv, cache_krt):
    nb, n_pages = page_table.shape
    n_groups = n_pages // DEC_PAGES
    assert n_pages % DEC_PAGES == 0 and n_groups >= DEC_AHEAD and n_groups % 2 == 0
    per_seq = lambda shape: pl.BlockSpec((1,) + shape, lambda i, pt: (jnp.minimum(i, nb - 1), 0, 0))
    const = lambda a: pl.BlockSpec(a.shape, lambda i, pt: (0,) * a.ndim, pipeline_mode=pl.Buffered(1))
    grid_spec = pltpu.PrefetchScalarGridSpec(
        num_scalar_prefetch=1,
        grid=(nb + 1,),
        in_specs=[const(lhsw), per_seq((QT_ROWS, KV_LORA)), per_seq((QT_ROWS, ROPE_DIM)),
                  per_seq((MLA_HEADS, 1)), per_seq((1, KV_LORA)), const(wuv),
                  pl.BlockSpec(memory_space=pl.ANY), pl.BlockSpec(memory_space=pl.ANY)],
        out_specs=pl.BlockSpec((1, 1, MLA_V_W), lambda i, pt: (jnp.maximum(i - 1, 0), 0, 0)),
        scratch_shapes=[pltpu.VMEM((UK_ROWS + QT_ROWS, KV_LORA), BF16),
                        pltpu.VMEM((DEC_SLOTS, DEC_T, KV_LORA), F32),
                        pltpu.VMEM((DEC_SLOTS, ROPE_DIM, DEC_T), F32),
                        pltpu.VMEM((2, DEC_T, KV_LORA), BF16),
                        pltpu.VMEM((UK_ROWS + QT_ROWS, DEC_T), F32),
                        pltpu.VMEM((MLA_HEADS, DEC_T), F32), pltpu.VMEM((MLA_HEADS, 1), F32),
                        pltpu.VMEM((MLA_HEADS, 1), F32), pltpu.VMEM((MLA_HEADS, KV_LORA), F32),
                        pltpu.SemaphoreType.DMA((2, DEC_SLOTS))],
    )
    return pl.pallas_call(
        _mla_decode_kernel,
        grid_spec=grid_spec,
        out_shape=jax.ShapeDtypeStruct((nb, 1, MLA_V_W), F32),
        compiler_params=_params(("arbitrary",)),
        name="mla_decode",
    )(page_table, lhsw, qt, qrope, s0, c_new, wuv, cache_ckv, cache_krt)


def _rotary_tables(pos):
    pos = pos.astype(F32)[:, None]

    def cos_sin(half):
        inv = ROPE_BASE ** (-jnp.arange(half, dtype=F32) / half)
        ang = pos * inv[None, :]
        return jnp.cos(ang), jnp.sin(ang)

    cr, sr = cos_sin(RET_DK // 2)
    cosr = jnp.concatenate([cr, cr], axis=-1)
    sinr = jnp.concatenate([-sr, sr], axis=-1)
    c, s = cos_sin(ROPE_DIM // 2)
    n = pos.shape[0]
    half = ROPE_DIM // 2
    z = lambda w: jnp.zeros((n, w), F32)
    cm = jnp.concatenate([jnp.ones((n, NOPE_DIM), F32), c, c, z(HEAD_PAD - QK_DIM)], axis=-1)
    sa = jnp.concatenate([z(NOPE_DIM), -s, z(half), z(HEAD_PAD - QK_DIM)], axis=-1)
    sb = jnp.concatenate([z(NOPE_DIM), z(half), s, z(HEAD_PAD - QK_DIM)], axis=-1)
    return cosr, sinr, cm, sa, sb


def _decay_tables(c):
    lg = jnp.log1p(-jnp.exp2(-5.0 - jnp.arange(RET_HEADS, dtype=F32)))
    idx = jnp.arange(c, dtype=F32)
    diff = idx[:, None] - idx[None, :]
    dec = jnp.where(diff[None] >= 0, jnp.exp(jnp.maximum(diff, 0.0)[None] * lg[:, None, None]), 0.0)
    qdec = jnp.exp((idx + 1.0)[None, :] * lg[:, None])[:, :, None]
    kdec = jnp.exp((c - 1.0 - idx)[None, :] * lg[:, None])[:, :, None]
    sdec = jnp.exp(c * lg)[:, None, None]
    return dec, qdec, kdec, sdec


def _pad_heads(a):
    a = jnp.pad(a, [(0, 0)] * (a.ndim - 1) + [(0, HEAD_PAD - a.shape[-1])])
    return a.reshape(a.shape[:-2] + (a.shape[-2] * HEAD_PAD,))


def _layer_weights(norm_mix, w_in, g_qa, w_qb, g_kva, w_uk, w_uv, g_qn, g_kn, w_ret_o, w_mla_o, w_out,
                   norm_ffn, w_up, w_down):
    idx = [int(i) for i in np.cumsum(SPLITS)[:-1]]
    parts = jnp.split(w_in, idx, axis=-1)
    parts[6] = jnp.pad(parts[6], ((0, 0), (NOPE_DIM, HEAD_PAD - QK_DIM)))
    pad1 = lambda g: jnp.pad(g, (0, HEAD_PAD - QK_DIM))[None, :]
    wukp = _pad_heads(w_uk)
    return {
        'norm_mix': norm_mix[None, :],
        'w_in': jnp.concatenate([parts[i] for i in _SEG_ORDER], axis=-1).astype(BF16),
        'g_qa': g_qa[None, :],
        'w_qb': _pad_heads(w_qb.reshape(Q_LORA, MLA_HEADS, QK_DIM)).astype(BF16),
        'g_kva': g_kva[None, :],
        'w_ukp': wukp.astype(BF16),
        'w_ukp_f32': wukp,
        'w_uk_t': w_uk.transpose(2, 1, 0).reshape(UK_ROWS, KV_LORA).astype(BF16),
        'w_uv': w_uv.reshape(KV_LORA, MLA_V_W).astype(BF16),
        'w_uv_t': w_uv.reshape(KV_LORA, MLA_V_W).T.astype(BF16),
        'gq': pad1(g_qn) * (QK_DIM ** -0.5 * LOG2_E),
        'gk': pad1(g_kn),
        'w_ret_o': w_ret_o.astype(BF16),
        'w_mla_o': w_mla_o.astype(BF16),
        'w_out': w_out.astype(BF16),
        'norm_ffn': norm_ffn[None, :],
        'w_up': w_up.astype(BF16),
        'w_down': w_down.astype(BF16),
    }


PROJ_TM = 256
FINISH_TM = 512


def kernel(x_prompt, x_sample, cache_ckv, cache_krope, state_ret, page_table, meta_tokens, norm_mix, w_in,
           g_qa, w_qb, g_kva, w_uk, w_uv, g_qn, g_kn, w_ret_o, w_mla_o, w_out, norm_ffn, w_up, w_down):
    nb, seq, _ = x_prompt.shape
    ns = x_sample.shape[0]
    assert x_sample.shape[1] == 1 and norm_mix.shape[0] == 1
    past = page_table.shape[1] * PAGE_SIZE
    w = _layer_weights(norm_mix[0], w_in[0], g_qa[0], w_qb[0], g_kva[0], w_uk[0], w_uv[0], g_qn[0], g_kn[0],
                       w_ret_o[0], w_mla_o[0], w_out[0], norm_ffn[0], w_up[0], w_down[0])
    dtabs = _decay_tables(RET_CHUNK)

    mpad = RET_CHUNK - N_META
    xm = jnp.pad(meta_tokens, ((mpad, 0), (0, 0)))
    pos_m = jnp.maximum(jnp.arange(RET_CHUNK) - mpad, 0)
    zero_state = jnp.zeros((1, RET_HEADS, RET_DK, RET_DV), F32)
    _, _, _, _, km, vmt, c_m, kr_m, s0 = _proj(xm, _rotary_tables(pos_m), w, RET_CHUNK, RET_CHUNK, BF16,
                                               retention=(RET_CHUNK, dtabs, zero_state))

    xp = x_prompt.reshape(nb * seq, D_MODEL)
    ret_mid, gr, gm, q, k, vt, c_p, kr_p, st_p = _proj(
        xp, _rotary_tables(N_META + jnp.arange(seq)), w, PROJ_TM, ATT_TK, BF16, retention=(seq, dtabs, s0))
    b3 = lambda a: a.reshape(nb, seq, a.shape[-1])
    o_mla = _attention(b3(q), b3(k), vt.reshape(nb, seq // ATT_TK, MLA_V_W, ATT_TK), km, vmt[0])
    y_prompt = _finish(xp, ret_mid, o_mla.reshape(nb * seq, MLA_V_W), gr, gm, w,
                       FINISH_TM).reshape(nb, seq, D_MODEL)
    bcast = lambda a: jnp.broadcast_to(a[None, mpad:], (nb, N_META, a.shape[-1]))
    ckv_prompt = jnp.concatenate([bcast(c_m), b3(c_p)], axis=1)[None]
    krope_prompt = jnp.concatenate([bcast(kr_m), b3(kr_p)], axis=1)[None]

    xs = x_sample.reshape(ns, D_MODEL)
    pos_s = jnp.full((ns,), past, jnp.int32)
    rq, rk, rv, rg, gr, gm, q, k, _, c_s, kr_s = _proj(xs, _rotary_tables(pos_s), w, ns, ns, F32)
    gam = jnp.exp(jnp.log1p(-jnp.exp2(-5.0 - jnp.arange(RET_HEADS, dtype=F32))))[:, None, None]
    ret_mid, st_s = _ret_decode(rq, rk, rv, rg, gam, state_ret[0])
    qt, qg, s0_new = _absorb(q, k, w['gk'], w['w_ukp_f32'])
    qt = jnp.pad(qt.transpose(1, 0, 2), ((0, 0), (0, QT_ROWS - MLA_HEADS), (0, 0))).astype(BF16)
    q_rope = qg.reshape(ns, MLA_HEADS, HEAD_PAD)[:, :, NOPE_DIM:QK_DIM]
    q_rope = jnp.pad(q_rope, ((0, 0), (0, QT_ROWS - MLA_HEADS), (0, 0))).astype(BF16)
    o_mla = _mla_decode(page_table, w['w_uk_t'], qt, q_rope, s0_new[:, :MLA_HEADS, None], c_s[:, None, :],
                        w['w_uv'], cache_ckv, jnp.swapaxes(cache_krope, 2, 3))
    y_sample = _finish(xs, ret_mid, o_mla.reshape(ns, MLA_V_W), gr, gm, w, ns).reshape(ns, 1, D_MODEL)

    return (y_prompt, y_sample, ckv_prompt, krope_prompt, st_p[None],
            c_s.reshape(1, ns, 1, KV_LORA), kr_s.reshape(1, ns, 1, ROPE_DIM), st_s[None])
```

```python
import functools

import numpy as np
import jax
import jax.numpy as jnp
from jax import lax
from jax.experimental import pallas as pl
from jax.experimental.pallas import tpu as pltpu

D_MODEL = 1024
N_META = 16
RET_HEADS = 4
RET_DK = 128
RET_DV = 256
RET_CHUNK = 128
MLA_HEADS = 8
Q_LORA = 384
KV_LORA = 256
NOPE_DIM = 64
ROPE_DIM = 32
QK_DIM = NOPE_DIM + ROPE_DIM
V_DIM = 64
D_FF = 4 * D_MODEL
PAGE_SIZE = 128
ROPE_BASE = 10000.0
EPS = 1e-6
LOG2_E = float(np.log2(np.e))

RET_QK_W = RET_HEADS * RET_DK
RET_V_W = RET_HEADS * RET_DV
MLA_V_W = MLA_HEADS * V_DIM
SPLITS = (RET_QK_W, RET_QK_W, RET_V_W, RET_V_W, Q_LORA, KV_LORA, ROPE_DIM, D_MODEL, D_MODEL)

LANES = 128
HEAD_PAD = LANES
MLA_PAD_W = MLA_HEADS * HEAD_PAD
_SEG_W = (RET_QK_W, RET_QK_W, RET_V_W, RET_V_W, Q_LORA, KV_LORA, LANES, D_MODEL, D_MODEL)
_SEG_ORDER = (4, 5, 6, 0, 1, 2, 3, 7, 8)
_SEG_START = dict(zip(_SEG_ORDER, np.cumsum([0] + [_SEG_W[i] for i in _SEG_ORDER[:-1]]).tolist()))
_SEG_O = tuple(_SEG_START[i] for i in range(len(_SEG_W)))
IN_PAD_W = sum(_SEG_W)
VMEM_LIMIT = 56 * 1024 * 1024

F32 = jnp.float32
BF16 = jnp.bfloat16
_NT = (((1,), (1,)), ((), ()))
_TN = (((0,), (0,)), ((), ()))


def _dot(a, b):
    return jnp.dot(a, b, preferred_element_type=F32)


def _dot_nt(a, b):
    return lax.dot_general(a, b, _NT, preferred_element_type=F32)


def _const_spec(shape):
    zeros = (0,) * len(shape)
    return pl.BlockSpec(shape, lambda *_: zeros, pipeline_mode=pl.Buffered(1))


def _params(sem):
    return pltpu.CompilerParams(dimension_semantics=sem, vmem_limit_bytes=VMEM_LIMIT)


_PROJ_IN = 15


def _proj_kernel(*refs, tiles_per_seq):
    (x_ref, cosr_ref, sinr_ref, cm_ref, sa_ref, sb_ref, nmix_ref, win_ref, gqa_ref,
     wqb_ref, gkva_ref, wukp_ref, wuvt_ref, gq_ref, gk_ref) = refs[:_PROJ_IN]
    if tiles_per_seq is None:
        rq_ref, rk_ref, rv_ref, rg_ref, gr_ref, gm_ref, q_ref, k_ref, vt_ref, c_ref, kr_ref = refs[_PROJ_IN:]
    else:
        dec_ref, qdec_ref, kdec_ref, sdec_ref, s0_ref = refs[_PROJ_IN:_PROJ_IN + 5]
        (ret_ref, gr_ref, gm_ref, q_ref, k_ref, vt_ref, c_ref, kr_ref, st_ref,
         rq_ref, rk_ref, rv_ref, rg_ref, s_scr) = refs[_PROJ_IN + 5:]
        tile = pl.program_id(0) % tiles_per_seq

        @pl.when(tile == 0)
        def _():
            s_scr[...] = s0_ref[0]

    x = x_ref[...]
    xn = x * lax.rsqrt(jnp.mean(x * x, axis=-1, keepdims=True) + EPS) * nmix_ref[...]
    xb = xn.astype(BF16)

    def seg(i):
        return _dot(xb, win_ref[:, _SEG_O[i]:_SEG_O[i] + _SEG_W[i]])

    cosr, sinr = cosr_ref[...], sinr_ref[...]
    cm, sa, sb = cm_ref[...], sa_ref[...], sb_ref[...]

    def ret_rotary(z, out_ref, scale):
        for h in range(RET_HEADS):
            zh = z[:, h * RET_DK:(h + 1) * RET_DK]
            r = zh * cosr + pltpu.roll(zh, RET_DK // 2, 1) * sinr
            if scale is not None:
                r = r * scale
            out_ref[:, h * RET_DK:(h + 1) * RET_DK] = r.astype(out_ref.dtype)

    def mla_rotary(zh):
        half = ROPE_DIM // 2
        return zh * cm + pltpu.roll(zh, HEAD_PAD - half, 1) * sa + pltpu.roll(zh, half, 1) * sb

    def head_norm(zh, g):
        ss = jnp.sum(zh * zh, axis=-1, keepdims=True)
        return zh * lax.rsqrt(ss * (1.0 / QK_DIM) + EPS) * g

    qa = seg(4)
    qn = qa * lax.rsqrt(jnp.mean(qa * qa, axis=-1, keepdims=True) + EPS) * gqa_ref[...]
    ckv = seg(5)
    c = ckv * lax.rsqrt(jnp.mean(ckv * ckv, axis=-1, keepdims=True) + EPS) * gkva_ref[...]
    c_ref[...] = c
    cb = c.astype(BF16)
    krp = mla_rotary(seg(6))
    kr_ref[...] = krp[:, NOPE_DIM:QK_DIM]

    ret_rotary(seg(0), rq_ref, None)
    ret_rotary(seg(1), rk_ref, RET_DK ** -0.5)
    rv_ref[...] = seg(2).astype(rv_ref.dtype)
    rg_ref[...] = seg(3).astype(rg_ref.dtype)

    qp = _dot(qn.astype(BF16), wqb_ref[...])
    gq = gq_ref[...]
    for h in range(MLA_HEADS):
        qh = mla_rotary(qp[:, h * HEAD_PAD:(h + 1) * HEAD_PAD])
        q_ref[:, h * HEAD_PAD:(h + 1) * HEAD_PAD] = head_norm(qh, gq).astype(q_ref.dtype)

    kp = _dot(cb, wukp_ref[...])
    gk = gk_ref[...]
    for h in range(MLA_HEADS):
        kh = kp[:, h * HEAD_PAD:(h + 1) * HEAD_PAD] + krp
        k_ref[:, h * HEAD_PAD:(h + 1) * HEAD_PAD] = head_norm(kh, gk).astype(k_ref.dtype)
    vt_w = vt_ref.shape[-1]
    for t in range(vt_ref.shape[0]):
        vt_ref[t] = _dot_nt(wuvt_ref[...], cb[t * vt_w:(t + 1) * vt_w]).astype(vt_ref.dtype)

    def branch_gates():
        gr_ref[...] = seg(7).astype(gr_ref.dtype)
        gm_ref[...] = seg(8).astype(gm_ref.dtype)

    if tiles_per_seq is None:
        branch_gates()
        return

    for ck in range(x_ref.shape[0] // RET_CHUNK):
        rows = slice(ck * RET_CHUNK, (ck + 1) * RET_CHUNK)
        for h in range(RET_HEADS):
            qh = rq_ref[rows, h * RET_DK:(h + 1) * RET_DK]
            kh = rk_ref[rows, h * RET_DK:(h + 1) * RET_DK]
            vh = rv_ref[rows, h * RET_DV:(h + 1) * RET_DV]
            scores = _dot_nt(qh, kh) * dec_ref[h]
            o = _dot(scores.astype(BF16), vh)
            s_prev = s_scr[h]
            o = o + _dot(qh, s_prev.astype(BF16)) * qdec_ref[h]
            kd = (kh.astype(F32) * kdec_ref[h]).astype(BF16)
            s_scr[h] = sdec_ref[h] * s_prev + lax.dot_general(kd, vh, _TN, preferred_element_type=F32)
            on = o * lax.rsqrt(jnp.mean(o * o, axis=-1, keepdims=True) + EPS)
            g = rg_ref[rows, h * RET_DV:(h + 1) * RET_DV].astype(F32)
            ret_ref[rows, h * RET_DV:(h + 1) * RET_DV] = (g * jax.nn.sigmoid(g) * on).astype(ret_ref.dtype)
    branch_gates()

    @pl.when(tile == tiles_per_seq - 1)
    def _():
        st_ref[0] = s_scr[...]


def _proj(x, tabs, w, tm, vt_w, mid_dtype, retention=None):
    rows = x.shape[0]
    period = tabs[0].shape[0] // tm
    row = lambda i: (i, 0)
    tab_spec = pl.BlockSpec((tm, LANES), lambda i: (i % period, 0))
    st_shape = (RET_HEADS, RET_DK, RET_DV)
    ret_widths = (RET_QK_W, RET_QK_W, RET_V_W, RET_V_W)
    widths = ((RET_V_W,) if retention else ret_widths) + (D_MODEL, D_MODEL, MLA_PAD_W, MLA_PAD_W)
    out_shape = [jax.ShapeDtypeStruct((rows, n), mid_dtype) for n in widths]
    out_shape += [jax.ShapeDtypeStruct((rows // vt_w, MLA_V_W, vt_w), mid_dtype),
                  jax.ShapeDtypeStruct((rows, KV_LORA), F32), jax.ShapeDtypeStruct((rows, ROPE_DIM), F32)]
    out_specs = [pl.BlockSpec((tm, n), row) for n in widths]
    out_specs += [pl.BlockSpec((tm // vt_w, MLA_V_W, vt_w), lambda i: (i, 0, 0)),
                  pl.BlockSpec((tm, KV_LORA), row), pl.BlockSpec((tm, ROPE_DIM), row)]
    consts = (w['norm_mix'], w['w_in'], w['g_qa'], w['w_qb'], w['g_kva'], w['w_ukp'], w['w_uv_t'],
              w['gq'], w['gk'])
    scratch, tiles_per_seq = [], None
    if retention:
        seq_rows, dtabs, s0 = retention
        tiles_per_seq = seq_rows // tm
        consts += tuple(dtabs) + (s0,)
        out_shape.append(jax.ShapeDtypeStruct((rows // seq_rows,) + st_shape, F32))
        out_specs.append(pl.BlockSpec((1,) + st_shape, lambda i: (i // tiles_per_seq, 0, 0, 0)))
        scratch = [pltpu.VMEM((tm, n), mid_dtype) for n in ret_widths] + [pltpu.VMEM(st_shape, F32)]
    return pl.pallas_call(
        functools.partial(_proj_kernel, tiles_per_seq=tiles_per_seq),
        grid=(rows // tm,),
        in_specs=[pl.BlockSpec((tm, D_MODEL), row)] + [tab_spec] * 5 + [_const_spec(a.shape) for a in consts],
        out_specs=out_specs,
        out_shape=out_shape,
        scratch_shapes=scratch,
        compiler_params=_params(("arbitrary" if retention else "parallel",)),
        name="proj",
    )(x, *tabs, *consts)


ATT_TQ = 512
ATT_TK = 256


def _attention_kernel(q_ref, k_ref, vt_ref, km_ref, vmt_ref, o_ref, qt_scr, s_scr, m_scr, l_scr, acc_scr):
    qi = pl.program_id(1)
    meta_rows = km_ref.shape[0]
    hs = lambda h: slice(h * HEAD_PAD, (h + 1) * HEAD_PAD)
    vs = lambda h: slice(h * V_DIM, (h + 1) * V_DIM)

    qt_scr[...] = q_ref[...].T
    m_scr[...] = jnp.full(m_scr.shape, -jnp.inf, F32)
    l_scr[...] = jnp.zeros(l_scr.shape, F32)
    acc_scr[...] = jnp.zeros(acc_scr.shape, F32)

    def update(n, keys, values_t, causal=False, c0=0):
        cols = slice(c0, ATT_TQ)
        for h in range(MLA_HEADS):
            s_scr[h, 0:n, cols] = _dot(keys(h), qt_scr[hs(h), cols])
        if causal:
            mask = (lax.broadcasted_iota(jnp.int32, (n, ATT_TQ - c0), 0)
                    <= lax.broadcasted_iota(jnp.int32, (n, ATT_TQ - c0), 1))
        for h in range(MLA_HEADS):
            s = s_scr[h, 0:n, cols]
            if causal:
                s = jnp.where(mask, s, -jnp.inf)
            m_old = m_scr[h, :, cols]
            m_new = jnp.maximum(m_old, jnp.max(s, axis=0, keepdims=True))
            alpha = jnp.exp2(m_old - m_new)
            p = jnp.exp2(s - m_new)
            m_scr[h, :, cols] = m_new
            l_scr[h, :, cols] = alpha * l_scr[h, :, cols] + jnp.sum(p, axis=0, keepdims=True)
            acc_scr[vs(h), cols] = alpha * acc_scr[vs(h), cols] + _dot(values_t(h), p.astype(BF16))

    update(meta_rows, lambda h: km_ref[:, hs(h)], lambda h: vmt_ref[vs(h), :])

    def key_tile(j, **kw):
        start = pl.multiple_of(j * ATT_TK, ATT_TK)
        update(ATT_TK, lambda h: k_ref[pl.ds(start, ATT_TK), hs(h)], lambda h: vt_ref[j, vs(h), :], **kw)

    def body(j, carry):
        key_tile(j)
        return carry

    first_diag = qi * (ATT_TQ // ATT_TK)
    lax.fori_loop(0, first_diag, body, 0)
    for d in range(ATT_TQ // ATT_TK):
        key_tile(first_diag + d, causal=True, c0=d * ATT_TK)

    for h in range(MLA_HEADS):
        acc_scr[vs(h), :] = acc_scr[vs(h), :] / l_scr[h]
    o_ref[...] = acc_scr[...].T.astype(o_ref.dtype)


def _attention(q, k, vt, km, vmt):
    nb, s = q.shape[:2]
    return pl.pallas_call(
        _attention_kernel,
        grid=(nb, s // ATT_TQ),
        in_specs=[pl.BlockSpec((None, ATT_TQ, MLA_PAD_W), lambda b, i: (b, i, 0)),
                  pl.BlockSpec((None, s, MLA_PAD_W), lambda b, i: (b, 0, 0)),
                  pl.BlockSpec((None, s // ATT_TK, MLA_V_W, ATT_TK), lambda b, i: (b, 0, 0, 0)),
                  _const_spec(km.shape), _const_spec(vmt.shape)],
        out_specs=pl.BlockSpec((None, ATT_TQ, MLA_V_W), lambda b, i: (b, i, 0)),
        out_shape=jax.ShapeDtypeStruct((nb, s, MLA_V_W), BF16),
        scratch_shapes=[pltpu.VMEM((MLA_PAD_W, ATT_TQ), BF16),
                        pltpu.VMEM((MLA_HEADS, ATT_TK, ATT_TQ), F32),
                        pltpu.VMEM((MLA_HEADS, 1, ATT_TQ), F32), pltpu.VMEM((MLA_HEADS, 1, ATT_TQ), F32),
                        pltpu.VMEM((MLA_V_W, ATT_TQ), F32)],
        compiler_params=_params(("parallel", "arbitrary")),
        name="attention",
    )(q, k, vt, km, vmt)


FF_BLOCK = 1024


def _finish_kernel(x_ref, ret_ref, mla_ref, gr_ref, gm_ref, wro_ref, wmo_ref, wout_ref, nffn_ref,
                   wup_ref, wdn_ref, y_ref):
    r = _dot(ret_ref[...].astype(BF16), wro_ref[...])
    m = _dot(mla_ref[...].astype(BF16), wmo_ref[...])
    mixed = jax.nn.sigmoid(gr_ref[...].astype(F32)) * r + jax.nn.sigmoid(gm_ref[...].astype(F32)) * m
    h = x_ref[...] + _dot(mixed.astype(BF16), wout_ref[...])
    hn = h * lax.rsqrt(jnp.mean(h * h, axis=-1, keepdims=True) + EPS) * nffn_ref[...]
    hb = hn.astype(BF16)
    y = h
    for f in range(D_FF // FF_BLOCK):
        u = jnp.maximum(_dot(hb, wup_ref[:, f * FF_BLOCK:(f + 1) * FF_BLOCK]), 0.0)
        y = y + _dot((u * u).astype(BF16), wdn_ref[f * FF_BLOCK:(f + 1) * FF_BLOCK, :])
    y_ref[...] = y


def _finish(x, ret_mid, o_mla, gr, gm, w, tm):
    rows = x.shape[0]
    row = lambda i: (i, 0)
    consts = (w['w_ret_o'], w['w_mla_o'], w['w_out'], w['norm_ffn'], w['w_up'], w['w_down'])
    return pl.pallas_call(
        _finish_kernel,
        grid=(rows // tm,),
        in_specs=[pl.BlockSpec((tm, D_MODEL), row), pl.BlockSpec((tm, RET_V_W), row),
                  pl.BlockSpec((tm, MLA_V_W), row), pl.BlockSpec((tm, D_MODEL), row),
                  pl.BlockSpec((tm, D_MODEL), row)] + [_const_spec(a.shape) for a in consts],
        out_specs=pl.BlockSpec((tm, D_MODEL), row),
        out_shape=jax.ShapeDtypeStruct((rows, D_MODEL), F32),
        compiler_params=_params(("parallel",)),
        name="finish",
    )(x, ret_mid, o_mla, gr, gm, *consts)


RET_DEC_GROUP = 8


def _column(row_vec, eye):
    return jnp.sum(jnp.where(eye, row_vec, 0.0), axis=-1, keepdims=True)


def _ret_decode_kernel(q_ref, k_ref, v_ref, rg_ref, gam_ref, s_ref, o_ref, sn_ref):
    eye = (lax.broadcasted_iota(jnp.int32, (RET_DK, RET_DK), 0)
           == lax.broadcasted_iota(jnp.int32, (RET_DK, RET_DK), 1))
    for b in range(RET_DEC_GROUP):
        for h in range(RET_HEADS):
            k_col = _column(k_ref[b:b + 1, h * RET_DK:(h + 1) * RET_DK].astype(F32), eye)
            v_row = v_ref[b:b + 1, h * RET_DV:(h + 1) * RET_DV].astype(F32)
            s_new = gam_ref[h] * s_ref[b, h] + k_col * v_row
            sn_ref[b, h] = s_new
            q8 = q_ref[:, h * RET_DK:(h + 1) * RET_DK].astype(BF16)
            o = _dot(q8, s_new.astype(BF16))[b:b + 1]
            on = o * lax.rsqrt(jnp.mean(o * o, axis=-1, keepdims=True) + EPS)
            g = rg_ref[b:b + 1, h * RET_DV:(h + 1) * RET_DV].astype(F32)
            o_ref[b:b + 1, h * RET_DV:(h + 1) * RET_DV] = (g * jax.nn.sigmoid(g) * on).astype(o_ref.dtype)


def _ret_decode(rq, rk, rv, rg, gam, state):
    nb = rq.shape[0]
    g = RET_DEC_GROUP
    row = lambda i: (i, 0)
    st_spec = pl.BlockSpec((g, RET_HEADS, RET_DK, RET_DV), lambda i: (i, 0, 0, 0))
    return pl.pallas_call(
        _ret_decode_kernel,
        grid=(nb // g,),
        in_specs=[pl.BlockSpec((g, RET_QK_W), row), pl.BlockSpec((g, RET_QK_W), row),
                  pl.BlockSpec((g, RET_V_W), row), pl.BlockSpec((g, RET_V_W), row),
                  _const_spec(gam.shape), st_spec],
        out_specs=[pl.BlockSpec((g, RET_V_W), row), st_spec],
        out_shape=[jax.ShapeDtypeStruct((nb, RET_V_W), F32), jax.ShapeDtypeStruct(state.shape, F32)],
        compiler_params=_params(("parallel",)),
        name="ret_decode",
    )(rq, rk, rv, rg, gam, state)


def _absorb_kernel(q_ref, k_ref, gk_ref, wukp_ref, qt_ref, qg_ref, s0_ref):
    q = q_ref[...]
    lane = lax.broadcasted_iota(jnp.int32, (q.shape[0], LANES), 1)
    s0 = jnp.zeros((q.shape[0], LANES), F32)
    gk = gk_ref[...]
    for h in range(MLA_HEADS):
        sl = slice(h * HEAD_PAD, (h + 1) * HEAD_PAD)
        qh = q[:, sl]
        s0 = s0 + jnp.where(lane == h, jnp.sum(qh * k_ref[:, sl], axis=-1, keepdims=True), 0.0)
        qg = qh * gk
        qg_ref[:, sl] = qg
        qt_ref[h] = lax.dot_general(qg, wukp_ref[:, sl], _NT, precision=lax.Precision.HIGHEST,
                                    preferred_element_type=F32)
    s0_ref[...] = s0


def _absorb(q, k, gk, wukp):
    nb = q.shape[0]
    return pl.pallas_call(
        _absorb_kernel,
        out_shape=[jax.ShapeDtypeStruct((MLA_HEADS, nb, KV_LORA), F32),
                   jax.ShapeDtypeStruct((nb, MLA_PAD_W), F32),
                   jax.ShapeDtypeStruct((nb, LANES), F32)],
        compiler_params=pltpu.CompilerParams(vmem_limit_bytes=VMEM_LIMIT),
        name="absorb",
    )(q, k, gk, wukp)


DEC_PAGES = 32
DEC_T = DEC_PAGES * PAGE_SIZE
DEC_SUB = 512
UK_ROWS = MLA_HEADS * NOPE_DIM
QT_ROWS = 16
DEC_AHEAD = 2
DEC_SLOTS = DEC_AHEAD + 1


def _mla_decode_kernel(pt_ref, lhsw_ref, qt_ref, qrope_ref, s0_ref, cnew_ref, wuv_ref, ckv_hbm, krt_hbm,
                       o_ref, lhs_scr, cbuf, kbuf, cb_scr, big_scr, s_st, m_st, l_st, acc_st, sem):
    b = pl.program_id(0)
    nb = pl.num_programs(0) - 1
    n_groups = pt_ref.shape[1] // DEC_PAGES

    def page_copies(seq, g, slot):
        copies = []
        for i in range(DEC_PAGES):
            page = pt_ref[seq, g * DEC_PAGES + i]
            tok = pl.ds(i * PAGE_SIZE, PAGE_SIZE)
            copies.append(pltpu.make_async_copy(ckv_hbm.at[0, page], cbuf.at[slot, tok], sem.at[0, slot]))
            copies.append(pltpu.make_async_copy(krt_hbm.at[0, page], kbuf.at[slot, :, tok], sem.at[1, slot]))
        return copies

    def start_group(seq, g, slot):
        for cp in page_copies(seq, g, slot):
            cp.start()

    def wait_group(slot):
        for cp in page_copies(0, 0, slot):
            cp.wait()

    def scores(slot, cb_slot):
        cb = cbuf[slot].astype(BF16)
        cb_scr[cb_slot] = cb
        big_scr[...] = _dot_nt(lhs_scr[...], cb)
        rope = _dot(qrope_ref[0], kbuf[slot].astype(BF16))
        pieces = []
        for t in range(DEC_T // DEC_SUB):
            tok = pl.ds(t * DEC_SUB, DEC_SUB)
            sq = [big_scr[d * MLA_HEADS:(d + 1) * MLA_HEADS, tok] for d in range(NOPE_DIM)]
            sq = [x * x for x in sq]
            while len(sq) > 1:
                sq = [sq[i] + sq[i + 1] for i in range(0, len(sq), 2)]
            krt = kbuf[slot, :, tok]
            kk = krt * krt
            kk = (kk[0:8] + kk[8:16]) + (kk[16:24] + kk[24:32])
            ss = sq[0] + jnp.sum(kk, axis=0, keepdims=True)
            num = big_scr[UK_ROWS:UK_ROWS + MLA_HEADS, tok] + rope[0:MLA_HEADS, t * DEC_SUB:(t + 1) * DEC_SUB]
            pieces.append(num * lax.rsqrt(ss * (1.0 / QK_DIM) + EPS))
        return jnp.concatenate(pieces, axis=-1)

    def softmax_update(s, slot, m, l, acc):
        m_new = jnp.maximum(m, jnp.max(s, axis=-1, keepdims=True))
        alpha = jnp.exp2(m - m_new)
        p = jnp.exp2(s - m_new)
        l = alpha * l + jnp.sum(p, axis=-1, keepdims=True)
        acc = alpha * acc + _dot(p.astype(BF16), cb_scr[slot])
        return m_new, l, acc

    last = nb * n_groups - 1

    def fetch(n):
        src = jnp.minimum(n, last)
        start_group(lax.div(src, n_groups), lax.rem(src, n_groups), lax.rem(n, DEC_SLOTS))

    def arrive(n):
        slot = lax.rem(n, DEC_SLOTS)
        wait_group(slot)
        return slot

    first = b * n_groups
    last_cb = (n_groups - 1) % 2

    def finish_previous():
        _, l, acc = softmax_update(s_st[...], last_cb, m_st[...], l_st[...], acc_st[...])
        lat = acc / l
        full = _dot(lat.astype(BF16), wuv_ref[...])
        own = (lax.broadcasted_iota(jnp.int32, full.shape, 1) // V_DIM
               == lax.broadcasted_iota(jnp.int32, full.shape, 0))
        o_ref[0] = jnp.sum(jnp.where(own, full, 0.0), axis=0, keepdims=True)

    @pl.when(b == 0)
    def _():
        for n in range(DEC_AHEAD):
            start_group(0, n, n)
        s_st[...] = jnp.zeros(s_st.shape, F32)
        m_st[...] = jnp.zeros(m_st.shape, F32)
        l_st[...] = jnp.ones(l_st.shape, F32)
        acc_st[...] = jnp.zeros(acc_st.shape, F32)
        cb_scr[last_cb] = jnp.zeros(cb_scr.shape[1:], BF16)

    @pl.when(b < nb)
    def _():
        lhs_scr[0:UK_ROWS, :] = lhsw_ref[...]
        lhs_scr[UK_ROWS:UK_ROWS + QT_ROWS, :] = qt_ref[0]

        s_first = scores(arrive(first), 0)
        fetch(first + DEC_AHEAD)
        finish_previous()

        def body(g, carry):
            s_prev, m, l, acc = carry
            s_cur = scores(arrive(first + g), g % 2)
            m, l, acc = softmax_update(s_prev, 1 - g % 2, m, l, acc)
            fetch(first + g + DEC_AHEAD)
            return s_cur, m, l, acc

        init = (s_first, s0_ref[0], jnp.ones((MLA_HEADS, 1), F32),
                jnp.broadcast_to(cnew_ref[0], (MLA_HEADS, KV_LORA)))
        s_st[...], m_st[...], l_st[...], acc_st[...] = lax.fori_loop(1, n_groups, body, init)

    @pl.when(b == nb)
    def _():
        finish_previous()
        for n in range(1, DEC_AHEAD + 1):
            wait_group(lax.rem(last + n, DEC_SLOTS))


def _mla_decode(page_table, lhsw, qt, qrope, s0, c_new, wuv, cache_ckv, cache_krt):
    nb, n_pages = page_table.shape
    n_groups = n_pages // DEC_PAGES
    assert n_pages % DEC_PAGES == 0 and n_groups >= DEC_AHEAD and n_groups % 2 == 0
    per_seq = lambda shape: pl.BlockSpec((1,) + shape, lambda i, pt: (jnp.minimum(i, nb - 1), 0, 0))
    const = lambda a: pl.BlockSpec(a.shape, lambda i, pt: (0,) * a.ndim, pipeline_mode=pl.Buffered(1))
    grid_spec = pltpu.PrefetchScalarGridSpec(
        num_scalar_prefetch=1,
        grid=(nb + 1,),
        in_specs=[const(lhsw), per_seq((QT_ROWS, KV_LORA)), per_seq((QT_ROWS, ROPE_DIM)),
                  per_seq((MLA_HEADS, 1)), per_seq((1, KV_LORA)), const(wuv),
                  pl.BlockSpec(memory_space=pl.ANY), pl.BlockSpec(memory_space=pl.ANY)],
        out_specs=pl.BlockSpec((1, 1, MLA_V_W), lambda i, pt: (jnp.maximum(i - 1, 0), 0, 0)),
        scratch_shapes=[pltpu.VMEM((UK_ROWS + QT_ROWS, KV_LORA), BF16),
                        pltpu.VMEM((DEC_SLOTS, DEC_T, KV_LORA), F32),
                        pltpu.VMEM((DEC_SLOTS, ROPE_DIM, DEC_T), F32),
                        pltpu.VMEM((2, DEC_T, KV_LORA), BF16),
                        pltpu.VMEM((UK_ROWS + QT_ROWS, DEC_T), F32),
                        pltpu.VMEM((MLA_HEADS, DEC_T), F32), pltpu.VMEM((MLA_HEADS, 1), F32),
                        pltpu.VMEM((MLA_HEADS, 1), F32), pltpu.VMEM((MLA_HEADS, KV_LORA), F32),
                        pltpu.SemaphoreType.DMA((2, DEC_SLOTS))],
    )
    return pl.pallas_call(
        _mla_decode_kernel,
        grid_spec=grid_spec,
        out_shape=jax.ShapeDtypeStruct((nb, 1, MLA_V_W), F32),
        compiler_params=_params(("arbitrary",)),
        name="mla_decode",
    )(page_table, lhsw, qt, qrope, s0, c_new, wuv, cache_ckv, cache_krt)


def _rotary_tables(pos):
    pos = pos.astype(F32)[:, None]

    def cos_sin(half):
        inv = ROPE_BASE ** (-jnp.arange(half, dtype=F32) / half)
        ang = pos * inv[None, :]
        return jnp.cos(ang), jnp.sin(ang)

    cr, sr = cos_sin(RET_DK // 2)
    cosr = jnp.concatenate([cr, cr], axis=-1)
    sinr = jnp.concatenate([-sr, sr], axis=-1)
    c, s = cos_sin(ROPE_DIM // 2)
    n = pos.shape[0]
    half = ROPE_DIM // 2
    z = lambda w: jnp.zeros((n, w), F32)
    cm = jnp.concatenate([jnp.ones((n, NOPE_DIM), F32), c, c, z(HEAD_PAD - QK_DIM)], axis=-1)
    sa = jnp.concatenate([z(NOPE_DIM), -s, z(half), z(HEAD_PAD - QK_DIM)], axis=-1)
    sb = jnp.concatenate([z(NOPE_DIM), z(half), s, z(HEAD_PAD - QK_DIM)], axis=-1)
    return cosr, sinr, cm, sa, sb


def _decay_tables(c):
    lg = jnp.log1p(-jnp.exp2(-5.0 - jnp.arange(RET_HEADS, dtype=F32)))
    idx = jnp.arange(c, dtype=F32)
    diff = idx[:, None] - idx[None, :]
    dec = jnp.where(diff[None] >= 0, jnp.exp(jnp.maximum(diff, 0.0)[None] * lg[:, None, None]), 0.0)
    qdec = jnp.exp((idx + 1.0)[None, :] * lg[:, None])[:, :, None]
    kdec = jnp.exp((c - 1.0 - idx)[None, :] * lg[:, None])[:, :, None]
    sdec = jnp.exp(c * lg)[:, None, None]
    return dec, qdec, kdec, sdec


def _pad_heads(a):
    a = jnp.pad(a, [(0, 0)] * (a.ndim - 1) + [(0, HEAD_PAD - a.shape[-1])])
    return a.reshape(a.shape[:-2] + (a.shape[-2] * HEAD_PAD,))


def _layer_weights(norm_mix, w_in, g_qa, w_qb, g_kva, w_uk, w_uv, g_qn, g_kn, w_ret_o, w_mla_o, w_out,
                   norm_ffn, w_up, w_down):
    idx = [int(i) for i in np.cumsum(SPLITS)[:-1]]
    parts = jnp.split(w_in, idx, axis=-1)
    parts[6] = jnp.pad(parts[6], ((0, 0), (NOPE_DIM, HEAD_PAD - QK_DIM)))
    pad1 = lambda g: jnp.pad(g, (0, HEAD_PAD - QK_DIM))[None, :]
    wukp = _pad_heads(w_uk)
    return {
        'norm_mix': norm_mix[None, :],
        'w_in': jnp.concatenate([parts[i] for i in _SEG_ORDER], axis=-1).astype(BF16),
        'g_qa': g_qa[None, :],
        'w_qb': _pad_heads(w_qb.reshape(Q_LORA, MLA_HEADS, QK_DIM)).astype(BF16),
        'g_kva': g_kva[None, :],
        'w_ukp': wukp.astype(BF16),
        'w_ukp_f32': wukp,
        'w_uk_t': w_uk.transpose(2, 1, 0).reshape(UK_ROWS, KV_LORA).astype(BF16),
        'w_uv': w_uv.reshape(KV_LORA, MLA_V_W).astype(BF16),
        'w_uv_t': w_uv.reshape(KV_LORA, MLA_V_W).T.astype(BF16),
        'gq': pad1(g_qn) * (QK_DIM ** -0.5 * LOG2_E),
        'gk': pad1(g_kn),
        'w_ret_o': w_ret_o.astype(BF16),
        'w_mla_o': w_mla_o.astype(BF16),
        'w_out': w_out.astype(BF16),
        'norm_ffn': norm_ffn[None, :],
        'w_up': w_up.astype(BF16),
        'w_down': w_down.astype(BF16),
    }


PROJ_TM = 256
FINISH_TM = 512


def kernel(x_prompt, x_sample, cache_ckv, cache_krope, state_ret, page_table, meta_tokens, norm_mix, w_in,
           g_qa, w_qb, g_kva, w_uk, w_uv, g_qn, g_kn, w_ret_o, w_mla_o, w_out, norm_ffn, w_up, w_down):
    nb, seq, _ = x_prompt.shape
    ns = x_sample.shape[0]
    assert x_sample.shape[1] == 1 and norm_mix.shape[0] == 1
    past = page_table.shape[1] * PAGE_SIZE
    w = _layer_weights(norm_mix[0], w_in[0], g_qa[0], w_qb[0], g_kva[0], w_uk[0], w_uv[0], g_qn[0], g_kn[0],
                       w_ret_o[0], w_mla_o[0], w_out[0], norm_ffn[0], w_up[0], w_down[0])
    dtabs = _decay_tables(RET_CHUNK)

    mpad = RET_CHUNK - N_META
    xm = jnp.pad(meta_tokens, ((mpad, 0), (0, 0)))
    pos_m = jnp.maximum(jnp.arange(RET_CHUNK) - mpad, 0)
    zero_state = jnp.zeros((1, RET_HEADS, RET_DK, RET_DV), F32)
    _, _, _, _, km, vmt, c_m, kr_m, s0 = _proj(xm, _rotary_tables(pos_m), w, RET_CHUNK, RET_CHUNK, BF16,
                                               retention=(RET_CHUNK, dtabs, zero_state))

    xp = x_prompt.reshape(nb * seq, D_MODEL)
    ret_mid, gr, gm, q, k, vt, c_p, kr_p, st_p = _proj(
        xp, _rotary_tables(N_META + jnp.arange(seq)), w, PROJ_TM, ATT_TK, BF16, retention=(seq, dtabs, s0))
    b3 = lambda a: a.reshape(nb, seq, a.shape[-1])
    o_mla = _attention(b3(q), b3(k), vt.reshape(nb, seq // ATT_TK, MLA_V_W, ATT_TK), km[mpad:],
                       vmt[0][:, mpad:])
    y_prompt = _finish(xp, ret_mid, o_mla.reshape(nb * seq, MLA_V_W), gr, gm, w,
                       FINISH_TM).reshape(nb, seq, D_MODEL)
    bcast = lambda a: jnp.broadcast_to(a[None, mpad:], (nb, N_META, a.shape[-1]))
    ckv_prompt = jnp.concatenate([bcast(c_m), b3(c_p)], axis=1)[None]
    krope_prompt = jnp.concatenate([bcast(kr_m), b3(kr_p)], axis=1)[None]

    xs = x_sample.reshape(ns, D_MODEL)
    pos_s = jnp.full((ns,), past, jnp.int32)
    rq, rk, rv, rg, gr, gm, q, k, _, c_s, kr_s = _proj(xs, _rotary_tables(pos_s), w, ns, ns, F32)
    gam = jnp.exp(jnp.log1p(-jnp.exp2(-5.0 - jnp.arange(RET_HEADS, dtype=F32))))[:, None, None]
    ret_mid, st_s = _ret_decode(rq, rk, rv, rg, gam, state_ret[0])
    qt, qg, s0_new = _absorb(q, k, w['gk'], w['w_ukp_f32'])
    qt = jnp.pad(qt.transpose(1, 0, 2), ((0, 0), (0, QT_ROWS - MLA_HEADS), (0, 0))).astype(BF16)
    q_rope = qg.reshape(ns, MLA_HEADS, HEAD_PAD)[:, :, NOPE_DIM:QK_DIM]
    q_rope = jnp.pad(q_rope, ((0, 0), (0, QT_ROWS - MLA_HEADS), (0, 0))).astype(BF16)
    o_mla = _mla_decode(page_table, w['w_uk_t'], qt, q_rope, s0_new[:, :MLA_HEADS, None], c_s[:, None, :],
                        w['w_uv'], cache_ckv, jnp.swapaxes(cache_krope, 2, 3))
    y_sample = _finish(xs, ret_mid, o_mla.reshape(ns, MLA_V_W), gr, gm, w, ns).reshape(ns, 1, D_MODEL)

    return (y_prompt, y_sample, ckv_prompt, krope_prompt, st_p[None],
            c_s.reshape(1, ns, 1, KV_LORA), kr_s.reshape(1, ns, 1, ROPE_DIM), st_s[None])
```

```python
import functools

import numpy as np
import jax
import jax.numpy as jnp
from jax import lax
from jax.experimental import pallas as pl
from jax.experimental.pallas import tpu as pltpu

D_MODEL = 1024
N_META = 16
RET_HEADS = 4
RET_DK = 128
RET_DV = 256
RET_CHUNK = 128
MLA_HEADS = 8
Q_LORA = 384
KV_LORA = 256
NOPE_DIM = 64
ROPE_DIM = 32
QK_DIM = NOPE_DIM + ROPE_DIM
V_DIM = 64
D_FF = 4 * D_MODEL
PAGE_SIZE = 128
ROPE_BASE = 10000.0
EPS = 1e-6
LOG2_E = float(np.log2(np.e))

RET_QK_W = RET_HEADS * RET_DK
RET_V_W = RET_HEADS * RET_DV
MLA_V_W = MLA_HEADS * V_DIM
SPLITS = (RET_QK_W, RET_QK_W, RET_V_W, RET_V_W, Q_LORA, KV_LORA, ROPE_DIM, D_MODEL, D_MODEL)

LANES = 128
HEAD_PAD = LANES
MLA_PAD_W = MLA_HEADS * HEAD_PAD
_SEG_W = (RET_QK_W, RET_QK_W, RET_V_W, RET_V_W, Q_LORA, KV_LORA, LANES, D_MODEL, D_MODEL)
_SEG_ORDER = (4, 5, 6, 0, 1, 2, 3, 7, 8)
_SEG_START = dict(zip(_SEG_ORDER, np.cumsum([0] + [_SEG_W[i] for i in _SEG_ORDER[:-1]]).tolist()))
_SEG_O = tuple(_SEG_START[i] for i in range(len(_SEG_W)))
VMEM_LIMIT = 56 * 1024 * 1024

F32 = jnp.float32
BF16 = jnp.bfloat16
_NT = (((1,), (1,)), ((), ()))
_TN = (((0,), (0,)), ((), ()))


def _dot(a, b):
    return jnp.dot(a, b, preferred_element_type=F32)


def _dot_nt(a, b):
    return lax.dot_general(a, b, _NT, preferred_element_type=F32)


def _const_spec(shape):
    zeros = (0,) * len(shape)
    return pl.BlockSpec(shape, lambda *_: zeros, pipeline_mode=pl.Buffered(1))


def _params(sem):
    return pltpu.CompilerParams(dimension_semantics=sem, vmem_limit_bytes=VMEM_LIMIT)


_PROJ_IN = 15


def _proj_kernel(*refs, tiles_per_seq):
    (x_ref, cosr_ref, sinr_ref, cm_ref, sa_ref, sb_ref, nmix_ref, win_ref, gqa_ref,
     wqb_ref, gkva_ref, wukp_ref, wuvt_ref, gq_ref, gk_ref) = refs[:_PROJ_IN]
    if tiles_per_seq is None:
        rq_ref, rk_ref, rv_ref, rg_ref, gr_ref, gm_ref, q_ref, k_ref, vt_ref, c_ref, kr_ref = refs[_PROJ_IN:]
    else:
        dec_ref, qdec_ref, kdec_ref, sdec_ref, s0_ref = refs[_PROJ_IN:_PROJ_IN + 5]
        (ret_ref, gr_ref, gm_ref, q_ref, k_ref, vt_ref, c_ref, kr_ref, st_ref,
         rq_ref, rk_ref, rv_ref, rg_ref, s_scr) = refs[_PROJ_IN + 5:]
        tile = pl.program_id(0) % tiles_per_seq

        @pl.when(tile == 0)
        def _():
            s_scr[...] = s0_ref[0]

    x = x_ref[...]
    xn = x * lax.rsqrt(jnp.mean(x * x, axis=-1, keepdims=True) + EPS) * nmix_ref[...]
    xb = xn.astype(BF16)

    def seg(i):
        return _dot(xb, win_ref[:, _SEG_O[i]:_SEG_O[i] + _SEG_W[i]])

    cosr, sinr = cosr_ref[...], sinr_ref[...]
    cm, sa, sb = cm_ref[...], sa_ref[...], sb_ref[...]

    def ret_rotary(z, out_ref, scale):
        for h in range(RET_HEADS):
            zh = z[:, h * RET_DK:(h + 1) * RET_DK]
            r = zh * cosr + pltpu.roll(zh, RET_DK // 2, 1) * sinr
            if scale is not None:
                r = r * scale
            out_ref[:, h * RET_DK:(h + 1) * RET_DK] = r.astype(out_ref.dtype)

    def mla_rotary(zh):
        half = ROPE_DIM // 2
        return zh * cm + pltpu.roll(zh, HEAD_PAD - half, 1) * sa + pltpu.roll(zh, half, 1) * sb

    def head_norm(zh, g):
        ss = jnp.sum(zh * zh, axis=-1, keepdims=True)
        return zh * lax.rsqrt(ss * (1.0 / QK_DIM) + EPS) * g

    qa = seg(4)
    qn = qa * lax.rsqrt(jnp.mean(qa * qa, axis=-1, keepdims=True) + EPS) * gqa_ref[...]
    ckv = seg(5)
    c = ckv * lax.rsqrt(jnp.mean(ckv * ckv, axis=-1, keepdims=True) + EPS) * gkva_ref[...]
    c_ref[...] = c
    cb = c.astype(BF16)
    krp = mla_rotary(seg(6))
    kr_ref[...] = krp[:, NOPE_DIM:QK_DIM]

    ret_rotary(seg(0), rq_ref, None)
    ret_rotary(seg(1), rk_ref, RET_DK ** -0.5)
    rv_ref[...] = seg(2).astype(rv_ref.dtype)
    rg_ref[...] = seg(3).astype(rg_ref.dtype)

    qp = _dot(qn.astype(BF16), wqb_ref[...])
    gr_ref[...] = seg(7).astype(gr_ref.dtype)
    gq = gq_ref[...]
    for h in range(MLA_HEADS):
        qh = mla_rotary(qp[:, h * HEAD_PAD:(h + 1) * HEAD_PAD])
        q_ref[:, h * HEAD_PAD:(h + 1) * HEAD_PAD] = head_norm(qh, gq).astype(q_ref.dtype)

    kp = _dot(cb, wukp_ref[...])
    gk = gk_ref[...]
    for h in range(MLA_HEADS):
        kh = kp[:, h * HEAD_PAD:(h + 1) * HEAD_PAD] + krp
        k_ref[:, h * HEAD_PAD:(h + 1) * HEAD_PAD] = head_norm(kh, gk).astype(k_ref.dtype)
    vt_w = vt_ref.shape[-1]
    for t in range(vt_ref.shape[0]):
        vt_ref[t] = _dot_nt(wuvt_ref[...], cb[t * vt_w:(t + 1) * vt_w]).astype(vt_ref.dtype)

    def branch_gates():
        gm_ref[...] = seg(8).astype(gm_ref.dtype)

    if tiles_per_seq is None:
        branch_gates()
        return

    for ck in range(x_ref.shape[0] // RET_CHUNK):
        rows = slice(ck * RET_CHUNK, (ck + 1) * RET_CHUNK)
        for h in range(RET_HEADS):
            qh = rq_ref[rows, h * RET_DK:(h + 1) * RET_DK]
            kh = rk_ref[rows, h * RET_DK:(h + 1) * RET_DK]
            vh = rv_ref[rows, h * RET_DV:(h + 1) * RET_DV]
            scores = _dot_nt(qh, kh) * dec_ref[h]
            o = _dot(scores.astype(BF16), vh)
            s_prev = s_scr[h]
            o = o + _dot(qh, s_prev.astype(BF16)) * qdec_ref[h]
            kd = (kh.astype(F32) * kdec_ref[h]).astype(BF16)
            s_scr[h] = sdec_ref[h] * s_prev + lax.dot_general(kd, vh, _TN, preferred_element_type=F32)
            on = o * lax.rsqrt(jnp.mean(o * o, axis=-1, keepdims=True) + EPS)
            g = rg_ref[rows, h * RET_DV:(h + 1) * RET_DV].astype(F32)
            ret_ref[rows, h * RET_DV:(h + 1) * RET_DV] = (g * jax.nn.sigmoid(g) * on).astype(ret_ref.dtype)
    branch_gates()

    @pl.when(tile == tiles_per_seq - 1)
    def _():
        st_ref[0] = s_scr[...]


def _proj(x, tabs, w, tm, vt_w, mid_dtype, retention=None):
    rows = x.shape[0]
    period = tabs[0].shape[0] // tm
    row = lambda i: (i, 0)
    tab_spec = pl.BlockSpec((tm, LANES), lambda i: (i % period, 0))
    st_shape = (RET_HEADS, RET_DK, RET_DV)
    ret_widths = (RET_QK_W, RET_QK_W, RET_V_W, RET_V_W)
    widths = ((RET_V_W,) if retention else ret_widths) + (D_MODEL, D_MODEL, MLA_PAD_W, MLA_PAD_W)
    out_shape = [jax.ShapeDtypeStruct((rows, n), mid_dtype) for n in widths]
    out_shape += [jax.ShapeDtypeStruct((rows // vt_w, MLA_V_W, vt_w), mid_dtype),
                  jax.ShapeDtypeStruct((rows, KV_LORA), F32), jax.ShapeDtypeStruct((rows, ROPE_DIM), F32)]
    out_specs = [pl.BlockSpec((tm, n), row) for n in widths]
    out_specs += [pl.BlockSpec((tm // vt_w, MLA_V_W, vt_w), lambda i: (i, 0, 0)),
                  pl.BlockSpec((tm, KV_LORA), row), pl.BlockSpec((tm, ROPE_DIM), row)]
    consts = (w['norm_mix'], w['w_in'], w['g_qa'], w['w_qb'], w['g_kva'], w['w_ukp'], w['w_uv_t'],
              w['gq'], w['gk'])
    scratch, tiles_per_seq = [], None
    if retention:
        seq_rows, dtabs, s0 = retention
        tiles_per_seq = seq_rows // tm
        consts += tuple(dtabs) + (s0,)
        out_shape.append(jax.ShapeDtypeStruct((rows // seq_rows,) + st_shape, F32))
        out_specs.append(pl.BlockSpec((1,) + st_shape, lambda i: (i // tiles_per_seq, 0, 0, 0)))
        scratch = [pltpu.VMEM((tm, n), mid_dtype) for n in ret_widths] + [pltpu.VMEM(st_shape, F32)]
    return pl.pallas_call(
        functools.partial(_proj_kernel, tiles_per_seq=tiles_per_seq),
        grid=(rows // tm,),
        in_specs=[pl.BlockSpec((tm, D_MODEL), row)] + [tab_spec] * 5 + [_const_spec(a.shape) for a in consts],
        out_specs=out_specs,
        out_shape=out_shape,
        scratch_shapes=scratch,
        compiler_params=_params(("arbitrary" if retention else "parallel",)),
        name="proj",
    )(x, *tabs, *consts)


ATT_TQ = 512
ATT_TK = 256


def _attention_kernel(q_ref, k_ref, vt_ref, km_ref, vmt_ref, o_ref, qt_scr, s_scr, m_scr, l_scr, acc_scr):
    qi = pl.program_id(1)
    meta_rows = km_ref.shape[0]
    hs = lambda h: slice(h * HEAD_PAD, (h + 1) * HEAD_PAD)
    vs = lambda h: slice(h * V_DIM, (h + 1) * V_DIM)

    qt_scr[...] = q_ref[...].T
    m_scr[...] = jnp.full(m_scr.shape, -jnp.inf, F32)
    l_scr[...] = jnp.zeros(l_scr.shape, F32)
    acc_scr[...] = jnp.zeros(acc_scr.shape, F32)

    def update(n, keys, values_t, causal=False, c0=0):
        cols = slice(c0, ATT_TQ)
        for h in range(MLA_HEADS):
            s_scr[h, 0:n, cols] = _dot(keys(h), qt_scr[hs(h), cols])
        if causal:
            mask = (lax.broadcasted_iota(jnp.int32, (n, ATT_TQ - c0), 0)
                    <= lax.broadcasted_iota(jnp.int32, (n, ATT_TQ - c0), 1))
        for h in range(MLA_HEADS):
            s = s_scr[h, 0:n, cols]
            if causal:
                s = jnp.where(mask, s, -jnp.inf)
            m_old = m_scr[h, :, cols]
            m_new = jnp.maximum(m_old, jnp.max(s, axis=0, keepdims=True))
            alpha = jnp.exp2(m_old - m_new)
            p = jnp.exp2(s - m_new)
            m_scr[h, :, cols] = m_new
            l_scr[h, :, cols] = alpha * l_scr[h, :, cols] + jnp.sum(p, axis=0, keepdims=True)
            acc_scr[vs(h), cols] = alpha * acc_scr[vs(h), cols] + _dot(values_t(h), p.astype(BF16))

    update(meta_rows, lambda h: km_ref[:, hs(h)], lambda h: vmt_ref[vs(h), :])

    def key_tile(j, **kw):
        start = pl.multiple_of(j * ATT_TK, ATT_TK)
        update(ATT_TK, lambda h: k_ref[pl.ds(start, ATT_TK), hs(h)], lambda h: vt_ref[j, vs(h), :], **kw)

    def body(j, carry):
        key_tile(j)
        return carry

    first_diag = qi * (ATT_TQ // ATT_TK)
    lax.fori_loop(0, first_diag, body, 0)
    for d in range(ATT_TQ // ATT_TK):
        key_tile(first_diag + d, causal=True, c0=d * ATT_TK)

    for h in range(MLA_HEADS):
        acc_scr[vs(h), :] = acc_scr[vs(h), :] / l_scr[h]
    o_ref[...] = acc_scr[...].T.astype(o_ref.dtype)


def _attention(q, k, vt, km, vmt):
    nb, s = q.shape[:2]
    return pl.pallas_call(
        _attention_kernel,
        grid=(nb, s // ATT_TQ),
        in_specs=[pl.BlockSpec((None, ATT_TQ, MLA_PAD_W), lambda b, i: (b, i, 0)),
                  pl.BlockSpec((None, s, MLA_PAD_W), lambda b, i: (b, 0, 0)),
                  pl.BlockSpec((None, s // ATT_TK, MLA_V_W, ATT_TK), lambda b, i: (b, 0, 0, 0)),
                  _const_spec(km.shape), _const_spec(vmt.shape)],
        out_specs=pl.BlockSpec((None, ATT_TQ, MLA_V_W), lambda b, i: (b, i, 0)),
        out_shape=jax.ShapeDtypeStruct((nb, s, MLA_V_W), BF16),
        scratch_shapes=[pltpu.VMEM((MLA_PAD_W, ATT_TQ), BF16),
                        pltpu.VMEM((MLA_HEADS, ATT_TK, ATT_TQ), F32),
                        pltpu.VMEM((MLA_HEADS, 1, ATT_TQ), F32), pltpu.VMEM((MLA_HEADS, 1, ATT_TQ), F32),
                        pltpu.VMEM((MLA_V_W, ATT_TQ), F32)],
        compiler_params=_params(("parallel", "arbitrary")),
        name="attention",
    )(q, k, vt, km, vmt)


FF_BLOCK = 1024


def _finish_kernel(x_ref, ret_ref, mla_ref, gr_ref, gm_ref, wro_ref, wmo_ref, wout_ref, nffn_ref,
                   wup_ref, wdn_ref, y_ref):
    r = _dot(ret_ref[...].astype(BF16), wro_ref[...])
    m = _dot(mla_ref[...].astype(BF16), wmo_ref[...])
    mixed = jax.nn.sigmoid(gr_ref[...].astype(F32)) * r + jax.nn.sigmoid(gm_ref[...].astype(F32)) * m
    h = x_ref[...] + _dot(mixed.astype(BF16), wout_ref[...])
    hn = h * lax.rsqrt(jnp.mean(h * h, axis=-1, keepdims=True) + EPS) * nffn_ref[...]
    hb = hn.astype(BF16)
    y = h
    for f in range(D_FF // FF_BLOCK):
        u = jnp.maximum(_dot(hb, wup_ref[:, f * FF_BLOCK:(f + 1) * FF_BLOCK]), 0.0)
        y = y + _dot((u * u).astype(BF16), wdn_ref[f * FF_BLOCK:(f + 1) * FF_BLOCK, :])
    y_ref[...] = y


def _finish(x, ret_mid, o_mla, gr, gm, w, tm):
    rows = x.shape[0]
    row = lambda i: (i, 0)
    consts = (w['w_ret_o'], w['w_mla_o'], w['w_out'], w['norm_ffn'], w['w_up'], w['w_down'])
    return pl.pallas_call(
        _finish_kernel,
        grid=(rows // tm,),
        in_specs=[pl.BlockSpec((tm, D_MODEL), row), pl.BlockSpec((tm, RET_V_W), row),
                  pl.BlockSpec((tm, MLA_V_W), row), pl.BlockSpec((tm, D_MODEL), row),
                  pl.BlockSpec((tm, D_MODEL), row)] + [_const_spec(a.shape) for a in consts],
        out_specs=pl.BlockSpec((tm, D_MODEL), row),
        out_shape=jax.ShapeDtypeStruct((rows, D_MODEL), F32),
        compiler_params=_params(("parallel",)),
        name="finish",
    )(x, ret_mid, o_mla, gr, gm, *consts)


RET_DEC_GROUP = 8


def _column(row_vec, eye):
    return jnp.sum(jnp.where(eye, row_vec, 0.0), axis=-1, keepdims=True)


def _ret_decode_kernel(q_ref, k_ref, v_ref, rg_ref, gam_ref, s_ref, o_ref, sn_ref):
    eye = (lax.broadcasted_iota(jnp.int32, (RET_DK, RET_DK), 0)
           == lax.broadcasted_iota(jnp.int32, (RET_DK, RET_DK), 1))
    for b in range(RET_DEC_GROUP):
        for h in range(RET_HEADS):
            k_col = _column(k_ref[b:b + 1, h * RET_DK:(h + 1) * RET_DK].astype(F32), eye)
            v_row = v_ref[b:b + 1, h * RET_DV:(h + 1) * RET_DV].astype(F32)
            s_new = gam_ref[h] * s_ref[b, h] + k_col * v_row
            sn_ref[b, h] = s_new
            q8 = q_ref[:, h * RET_DK:(h + 1) * RET_DK].astype(BF16)
            o = _dot(q8, s_new.astype(BF16))[b:b + 1]
            on = o * lax.rsqrt(jnp.mean(o * o, axis=-1, keepdims=True) + EPS)
            g = rg_ref[b:b + 1, h * RET_DV:(h + 1) * RET_DV].astype(F32)
            o_ref[b:b + 1, h * RET_DV:(h + 1) * RET_DV] = (g * jax.nn.sigmoid(g) * on).astype(o_ref.dtype)


def _ret_decode(rq, rk, rv, rg, gam, state):
    nb = rq.shape[0]
    g = RET_DEC_GROUP
    row = lambda i: (i, 0)
    st_spec = pl.BlockSpec((g, RET_HEADS, RET_DK, RET_DV), lambda i: (i, 0, 0, 0))
    return pl.pallas_call(
        _ret_decode_kernel,
        grid=(nb // g,),
        in_specs=[pl.BlockSpec((g, RET_QK_W), row), pl.BlockSpec((g, RET_QK_W), row),
                  pl.BlockSpec((g, RET_V_W), row), pl.BlockSpec((g, RET_V_W), row),
                  _const_spec(gam.shape), st_spec],
        out_specs=[pl.BlockSpec((g, RET_V_W), row), st_spec],
        out_shape=[jax.ShapeDtypeStruct((nb, RET_V_W), F32), jax.ShapeDtypeStruct(state.shape, F32)],
        compiler_params=_params(("parallel",)),
        name="ret_decode",
    )(rq, rk, rv, rg, gam, state)


def _absorb_kernel(q_ref, k_ref, gk_ref, wukp_ref, qt_ref, qg_ref, s0_ref):
    q = q_ref[...]
    lane = lax.broadcasted_iota(jnp.int32, (q.shape[0], LANES), 1)
    s0 = jnp.zeros((q.shape[0], LANES), F32)
    gk = gk_ref[...]
    for h in range(MLA_HEADS):
        sl = slice(h * HEAD_PAD, (h + 1) * HEAD_PAD)
        qh = q[:, sl]
        s0 = s0 + jnp.where(lane == h, jnp.sum(qh * k_ref[:, sl], axis=-1, keepdims=True), 0.0)
        qg = qh * gk
        qg_ref[:, sl] = qg
        qt_ref[h] = lax.dot_general(qg, wukp_ref[:, sl], _NT, precision=lax.Precision.HIGHEST,
                                    preferred_element_type=F32)
    s0_ref[...] = s0


def _absorb(q, k, gk, wukp):
    nb = q.shape[0]
    return pl.pallas_call(
        _absorb_kernel,
        out_shape=[jax.ShapeDtypeStruct((MLA_HEADS, nb, KV_LORA), F32),
                   jax.ShapeDtypeStruct((nb, MLA_PAD_W), F32),
                   jax.ShapeDtypeStruct((nb, LANES), F32)],
        compiler_params=pltpu.CompilerParams(vmem_limit_bytes=VMEM_LIMIT),
        name="absorb",
    )(q, k, gk, wukp)


DEC_PAGES = 32
DEC_T = DEC_PAGES * PAGE_SIZE
DEC_SUB = 512
UK_ROWS = MLA_HEADS * NOPE_DIM
QT_ROWS = 16
DEC_AHEAD = 2
DEC_SLOTS = DEC_AHEAD + 1


def _mla_decode_kernel(pt_ref, lhsw_ref, qt_ref, qrope_ref, s0_ref, cnew_ref, wuv_ref, ckv_hbm, krt_hbm,
                       o_ref, lhs_scr, cbuf, kbuf, cb_scr, big_scr, s_st, m_st, l_st, acc_st, sem):
    b = pl.program_id(0)
    nb = pl.num_programs(0) - 1
    n_groups = pt_ref.shape[1] // DEC_PAGES

    def page_copies(seq, g, slot):
        copies = []
        for i in range(DEC_PAGES):
            page = pt_ref[seq, g * DEC_PAGES + i]
            tok = pl.ds(i * PAGE_SIZE, PAGE_SIZE)
            copies.append(pltpu.make_async_copy(ckv_hbm.at[0, page], cbuf.at[slot, tok], sem.at[0, slot]))
            copies.append(pltpu.make_async_copy(krt_hbm.at[0, page], kbuf.at[slot, :, tok], sem.at[1, slot]))
        return copies

    def start_group(seq, g, slot):
        for cp in page_copies(seq, g, slot):
            cp.start()

    def wait_group(slot):
        for cp in page_copies(0, 0, slot):
            cp.wait()

    def scores(slot, cb_slot):
        cb = cbuf[slot].astype(BF16)
        cb_scr[cb_slot] = cb
        big_scr[...] = _dot_nt(lhs_scr[...], cb)
        rope = _dot(qrope_ref[0], kbuf[slot].astype(BF16))
        pieces = []
        for t in range(DEC_T // DEC_SUB):
            tok = pl.ds(t * DEC_SUB, DEC_SUB)
            sq = [big_scr[d * MLA_HEADS:(d + 1) * MLA_HEADS, tok] for d in range(NOPE_DIM)]
            sq = [x * x for x in sq]
            while len(sq) > 1:
                sq = [sq[i] + sq[i + 1] for i in range(0, len(sq), 2)]
            krt = kbuf[slot, :, tok]
            kk = krt * krt
            kk = (kk[0:8] + kk[8:16]) + (kk[16:24] + kk[24:32])
            ss = sq[0] + jnp.sum(kk, axis=0, keepdims=True)
            num = big_scr[UK_ROWS:UK_ROWS + MLA_HEADS, tok] + rope[0:MLA_HEADS, t * DEC_SUB:(t + 1) * DEC_SUB]
            pieces.append(num * lax.rsqrt(ss * (1.0 / QK_DIM) + EPS))
        return jnp.concatenate(pieces, axis=-1)

    def softmax_update(s, slot, m, l, acc):
        m_new = jnp.maximum(m, jnp.max(s, axis=-1, keepdims=True))
        alpha = jnp.exp2(m - m_new)
        p = jnp.exp2(s - m_new)
        l = alpha * l + jnp.sum(p, axis=-1, keepdims=True)
        acc = alpha * acc + _dot(p.astype(BF16), cb_scr[slot])
        return m_new, l, acc

    last = nb * n_groups - 1

    def fetch(n):
        src = jnp.minimum(n, last)
        start_group(lax.div(src, n_groups), lax.rem(src, n_groups), lax.rem(n, DEC_SLOTS))

    def arrive(n):
        slot = lax.rem(n, DEC_SLOTS)
        wait_group(slot)
        return slot

    first = b * n_groups
    last_cb = (n_groups - 1) % 2

    def finish_previous():
        _, l, acc = softmax_update(s_st[...], last_cb, m_st[...], l_st[...], acc_st[...])
        lat = acc / l
        full = _dot(lat.astype(BF16), wuv_ref[...])
        own = (lax.broadcasted_iota(jnp.int32, full.shape, 1) // V_DIM
               == lax.broadcasted_iota(jnp.int32, full.shape, 0))
        o_ref[0] = jnp.sum(jnp.where(own, full, 0.0), axis=0, keepdims=True)

    @pl.when(b == 0)
    def _():
        for n in range(DEC_AHEAD):
            start_group(0, n, n)
        s_st[...] = jnp.zeros(s_st.shape, F32)
        m_st[...] = jnp.zeros(m_st.shape, F32)
        l_st[...] = jnp.ones(l_st.shape, F32)
        acc_st[...] = jnp.zeros(acc_st.shape, F32)
        cb_scr[last_cb] = jnp.zeros(cb_scr.shape[1:], BF16)

    @pl.when(b < nb)
    def _():
        lhs_scr[0:UK_ROWS, :] = lhsw_ref[...]
        lhs_scr[UK_ROWS:UK_ROWS + QT_ROWS, :] = qt_ref[0]

        s_first = scores(arrive(first), 0)
        fetch(first + DEC_AHEAD)
        finish_previous()

        def body(g, carry):
            s_prev, m, l, acc = carry
            s_cur = scores(arrive(first + g), g % 2)
            m, l, acc = softmax_update(s_prev, 1 - g % 2, m, l, acc)
            fetch(first + g + DEC_AHEAD)
            return s_cur, m, l, acc

        init = (s_first, s0_ref[0], jnp.ones((MLA_HEADS, 1), F32),
                jnp.broadcast_to(cnew_ref[0], (MLA_HEADS, KV_LORA)))
        s_st[...], m_st[...], l_st[...], acc_st[...] = lax.fori_loop(1, n_groups, body, init)

    @pl.when(b == nb)
    def _():
        finish_previous()
        for n in range(1, DEC_AHEAD + 1):
            wait_group(lax.rem(last + n, DEC_SLOTS))


def _mla_decode(page_table, lhsw, qt, qrope, s0, c_new, wuv, cache_ckv, cache_krt):
    nb, n_pages = page_table.shape
    n_groups = n_pages // DEC_PAGES
    assert n_pages % DEC_PAGES == 0 and n_groups >= DEC_AHEAD and n_groups % 2 == 0
    per_seq = lambda shape: pl.BlockSpec((1,) + shape, lambda i, pt: (jnp.minimum(i, nb - 1), 0, 0))
    const = lambda a: pl.BlockSpec(a.shape, lambda i, pt: (0,) * a.ndim, pipeline_mode=pl.Buffered(1))
    grid_spec = pltpu.PrefetchScalarGridSpec(
        num_scalar_prefetch=1,
        grid=(nb + 1,),
        in_specs=[const(lhsw), per_seq((QT_ROWS, KV_LORA)), per_seq((QT_ROWS, ROPE_DIM)),
                  per_seq((MLA_HEADS, 1)), per_seq((1, KV_LORA)), const(wuv),
                  pl.BlockSpec(memory_space=pl.ANY), pl.BlockSpec(memory_space=pl.ANY)],
        out_specs=pl.BlockSpec((1, 1, MLA_V_W), lambda i, pt: (jnp.maximum(i - 1, 0), 0, 0)),
        scratch_shapes=[pltpu.VMEM((UK_ROWS + QT_ROWS, KV_LORA), BF16),
                        pltpu.VMEM((DEC_SLOTS, DEC_T, KV_LORA), F32),
                        pltpu.VMEM((DEC_SLOTS, ROPE_DIM, DEC_T), F32),
                        pltpu.VMEM((2, DEC_T, KV_LORA), BF16),
                        pltpu.VMEM((UK_ROWS + QT_ROWS, DEC_T), F32),
                        pltpu.VMEM((MLA_HEADS, DEC_T), F32), pltpu.VMEM((MLA_HEADS, 1), F32),
                        pltpu.VMEM((MLA_HEADS, 1), F32), pltpu.VMEM((MLA_HEADS, KV_LORA), F32),
                        pltpu.SemaphoreType.DMA((2, DEC_SLOTS))],
    )
    return pl.pallas_call(
        _mla_decode_kernel,
        grid_spec=grid_spec,
        out_shape=jax.ShapeDtypeStruct((nb, 1, MLA_V_W), F32),
        compiler_params=_params(("arbitrary",)),
        name="mla_decode",
    )(page_table, lhsw, qt, qrope, s0, c_new, wuv, cache_ckv, cache_krt)


def _rotary_tables(pos):
    pos = pos.astype(F32)[:, None]

    def cos_sin(half):
        inv = ROPE_BASE ** (-jnp.arange(half, dtype=F32) / half)
        ang = pos * inv[None, :]
        return jnp.cos(ang), jnp.sin(ang)

    cr, sr = cos_sin(RET_DK // 2)
    cosr = jnp.concatenate([cr, cr], axis=-1)
    sinr = jnp.concatenate([-sr, sr], axis=-1)
    c, s = cos_sin(ROPE_DIM // 2)
    n = pos.shape[0]
    half = ROPE_DIM // 2
    z = lambda w: jnp.zeros((n, w), F32)
    cm = jnp.concatenate([jnp.ones((n, NOPE_DIM), F32), c, c, z(HEAD_PAD - QK_DIM)], axis=-1)
    sa = jnp.concatenate([z(NOPE_DIM), -s, z(half), z(HEAD_PAD - QK_DIM)], axis=-1)
    sb = jnp.concatenate([z(NOPE_DIM), z(half), s, z(HEAD_PAD - QK_DIM)], axis=-1)
    return cosr, sinr, cm, sa, sb


def _decay_tables(c):
    lg = jnp.log1p(-jnp.exp2(-5.0 - jnp.arange(RET_HEADS, dtype=F32)))
    idx = jnp.arange(c, dtype=F32)
    diff = idx[:, None] - idx[None, :]
    dec = jnp.where(diff[None] >= 0, jnp.exp(jnp.maximum(diff, 0.0)[None] * lg[:, None, None]), 0.0)
    qdec = jnp.exp((idx + 1.0)[None, :] * lg[:, None])[:, :, None]
    kdec = jnp.exp((c - 1.0 - idx)[None, :] * lg[:, None])[:, :, None]
    sdec = jnp.exp(c * lg)[:, None, None]
    return dec, qdec, kdec, sdec


def _pad_heads(a):
    a = jnp.pad(a, [(0, 0)] * (a.ndim - 1) + [(0, HEAD_PAD - a.shape[-1])])
    return a.reshape(a.shape[:-2] + (a.shape[-2] * HEAD_PAD,))


def _layer_weights(norm_mix, w_in, g_qa, w_qb, g_kva, w_uk, w_uv, g_qn, g_kn, w_ret_o, w_mla_o, w_out,
                   norm_ffn, w_up, w_down):
    idx = [int(i) for i in np.cumsum(SPLITS)[:-1]]
    parts = jnp.split(w_in, idx, axis=-1)
    parts[6] = jnp.pad(parts[6], ((0, 0), (NOPE_DIM, HEAD_PAD - QK_DIM)))
    pad1 = lambda g: jnp.pad(g, (0, HEAD_PAD - QK_DIM))[None, :]
    wukp = _pad_heads(w_uk)
    return {
        'norm_mix': norm_mix[None, :],
        'w_in': jnp.concatenate([parts[i] for i in _SEG_ORDER], axis=-1).astype(BF16),
        'g_qa': g_qa[None, :],
        'w_qb': _pad_heads(w_qb.reshape(Q_LORA, MLA_HEADS, QK_DIM)).astype(BF16),
        'g_kva': g_kva[None, :],
        'w_ukp': wukp.astype(BF16),
        'w_ukp_f32': wukp,
        'w_uk_t': w_uk.transpose(2, 1, 0).reshape(UK_ROWS, KV_LORA).astype(BF16),
        'w_uv': w_uv.reshape(KV_LORA, MLA_V_W).astype(BF16),
        'w_uv_t': w_uv.reshape(KV_LORA, MLA_V_W).T.astype(BF16),
        'gq': pad1(g_qn) * (QK_DIM ** -0.5 * LOG2_E),
        'gk': pad1(g_kn),
        'w_ret_o': w_ret_o.astype(BF16),
        'w_mla_o': w_mla_o.astype(BF16),
        'w_out': w_out.astype(BF16),
        'norm_ffn': norm_ffn[None, :],
        'w_up': w_up.astype(BF16),
        'w_down': w_down.astype(BF16),
    }


PROJ_TM = 256
FINISH_TM = 512


def kernel(x_prompt, x_sample, cache_ckv, cache_krope, state_ret, page_table, meta_tokens, norm_mix, w_in,
           g_qa, w_qb, g_kva, w_uk, w_uv, g_qn, g_kn, w_ret_o, w_mla_o, w_out, norm_ffn, w_up, w_down):
    nb, seq, _ = x_prompt.shape
    ns = x_sample.shape[0]
    assert x_sample.shape[1] == 1 and norm_mix.shape[0] == 1
    past = page_table.shape[1] * PAGE_SIZE
    w = _layer_weights(norm_mix[0], w_in[0], g_qa[0], w_qb[0], g_kva[0], w_uk[0], w_uv[0], g_qn[0], g_kn[0],
                       w_ret_o[0], w_mla_o[0], w_out[0], norm_ffn[0], w_up[0], w_down[0])
    dtabs = _decay_tables(RET_CHUNK)

    mpad = RET_CHUNK - N_META
    xm = jnp.pad(meta_tokens, ((mpad, 0), (0, 0)))
    pos_m = jnp.maximum(jnp.arange(RET_CHUNK) - mpad, 0)
    zero_state = jnp.zeros((1, RET_HEADS, RET_DK, RET_DV), F32)
    _, _, _, _, km, vmt, c_m, kr_m, s0 = _proj(xm, _rotary_tables(pos_m), w, RET_CHUNK, RET_CHUNK, BF16,
                                               retention=(RET_CHUNK, dtabs, zero_state))

    xp = x_prompt.reshape(nb * seq, D_MODEL)
    ret_mid, gr, gm, q, k, vt, c_p, kr_p, st_p = _proj(
        xp, _rotary_tables(N_META + jnp.arange(seq)), w, PROJ_TM, ATT_TK, BF16, retention=(seq, dtabs, s0))
    b3 = lambda a: a.reshape(nb, seq, a.shape[-1])
    o_mla = _attention(b3(q), b3(k), vt.reshape(nb, seq // ATT_TK, MLA_V_W, ATT_TK), km[mpad:],
                       vmt[0][:, mpad:])
    y_prompt = _finish(xp, ret_mid, o_mla.reshape(nb * seq, MLA_V_W), gr, gm, w,
                       FINISH_TM).reshape(nb, seq, D_MODEL)
    bcast = lambda a: jnp.broadcast_to(a[None, mpad:], (nb, N_META, a.shape[-1]))
    ckv_prompt = jnp.concatenate([bcast(c_m), b3(c_p)], axis=1)[None]
    krope_prompt = jnp.concatenate([bcast(kr_m), b3(kr_p)], axis=1)[None]

    xs = x_sample.reshape(ns, D_MODEL)
    pos_s = jnp.full((ns,), past, jnp.int32)
    rq, rk, rv, rg, gr, gm, q, k, _, c_s, kr_s = _proj(xs, _rotary_tables(pos_s), w, ns, ns, F32)
    gam = jnp.exp(jnp.log1p(-jnp.exp2(-5.0 - jnp.arange(RET_HEADS, dtype=F32))))[:, None, None]
    ret_mid, st_s = _ret_decode(rq, rk, rv, rg, gam, state_ret[0])
    qt, qg, s0_new = _absorb(q, k, w['gk'], w['w_ukp_f32'])
    qt = jnp.pad(qt.transpose(1, 0, 2), ((0, 0), (0, QT_ROWS - MLA_HEADS), (0, 0))).astype(BF16)
    q_rope = qg.reshape(ns, MLA_HEADS, HEAD_PAD)[:, :, NOPE_DIM:QK_DIM]
    q_rope = jnp.pad(q_rope, ((0, 0), (0, QT_ROWS - MLA_HEADS), (0, 0))).astype(BF16)
    o_mla = _mla_decode(page_table, w['w_uk_t'], qt, q_rope, s0_new[:, :MLA_HEADS, None], c_s[:, None, :],
                        w['w_uv'], cache_ckv, jnp.swapaxes(cache_krope, 2, 3))
    y_sample = _finish(xs, ret_mid, o_mla.reshape(ns, MLA_V_W), gr, gm, w, ns).reshape(ns, 1, D_MODEL)

    return (y_prompt, y_sample, ckv_prompt, krope_prompt, st_p[None],
            c_s.reshape(1, ns, 1, KV_LORA), kr_s.reshape(1, ns, 1, ROPE_DIM), st_s[None])
```

```python
import functools

import numpy as np
import jax
import jax.numpy as jnp
from jax import lax
from jax.experimental import pallas as pl
from jax.experimental.pallas import tpu as pltpu

D_MODEL = 1024
N_META = 16
RET_HEADS = 4
RET_DK = 128
RET_DV = 256
RET_CHUNK = 128
MLA_HEADS = 8
Q_LORA = 384
KV_LORA = 256
NOPE_DIM = 64
ROPE_DIM = 32
QK_DIM = NOPE_DIM + ROPE_DIM
V_DIM = 64
D_FF = 4 * D_MODEL
PAGE_SIZE = 128
ROPE_BASE = 10000.0
EPS = 1e-6
LOG2_E = float(np.log2(np.e))

RET_QK_W = RET_HEADS * RET_DK
RET_V_W = RET_HEADS * RET_DV
MLA_V_W = MLA_HEADS * V_DIM
SPLITS = (RET_QK_W, RET_QK_W, RET_V_W, RET_V_W, Q_LORA, KV_LORA, ROPE_DIM, D_MODEL, D_MODEL)

LANES = 128
HEAD_PAD = LANES
MLA_PAD_W = MLA_HEADS * HEAD_PAD
_SEG_W = (RET_QK_W, RET_QK_W, RET_V_W, RET_V_W, Q_LORA, KV_LORA, LANES, D_MODEL, D_MODEL)
_SEG_ORDER = (4, 5, 6, 0, 1, 2, 3, 7, 8)
_SEG_START = dict(zip(_SEG_ORDER, np.cumsum([0] + [_SEG_W[i] for i in _SEG_ORDER[:-1]]).tolist()))
_SEG_O = tuple(_SEG_START[i] for i in range(len(_SEG_W)))
VMEM_LIMIT = 56 * 1024 * 1024

F32 = jnp.float32
BF16 = jnp.bfloat16
_NT = (((1,), (1,)), ((), ()))
_TN = (((0,), (0,)), ((), ()))


def _dot(a, b):
    return jnp.dot(a, b, preferred_element_type=F32)


def _dot_nt(a, b):
    return lax.dot_general(a, b, _NT, preferred_element_type=F32)


def _const_spec(shape):
    zeros = (0,) * len(shape)
    return pl.BlockSpec(shape, lambda *_: zeros, pipeline_mode=pl.Buffered(1))


def _params(sem):
    return pltpu.CompilerParams(dimension_semantics=sem, vmem_limit_bytes=VMEM_LIMIT)


_PROJ_IN = 15


def _proj_kernel(*refs, tiles_per_seq):
    (x_ref, cosr_ref, sinr_ref, cm_ref, sa_ref, sb_ref, nmix_ref, win_ref, gqa_ref,
     wqb_ref, gkva_ref, wukp_ref, wuvt_ref, gq_ref, gk_ref) = refs[:_PROJ_IN]
    if tiles_per_seq is None:
        rq_ref, rk_ref, rv_ref, rg_ref, gr_ref, gm_ref, q_ref, k_ref, vt_ref, c_ref, kr_ref = refs[_PROJ_IN:]
    else:
        dec_ref, qdec_ref, kdec_ref, sdec_ref, s0_ref = refs[_PROJ_IN:_PROJ_IN + 5]
        (ret_ref, gr_ref, gm_ref, q_ref, k_ref, vt_ref, c_ref, kr_ref, st_ref,
         rq_ref, rk_ref, rv_ref, rg_ref, s_scr) = refs[_PROJ_IN + 5:]
        tile = pl.program_id(0) % tiles_per_seq

        @pl.when(tile == 0)
        def _():
            s_scr[...] = s0_ref[0]

    x = x_ref[...]
    xn = x * lax.rsqrt(jnp.mean(x * x, axis=-1, keepdims=True) + EPS) * nmix_ref[...]
    xb = xn.astype(BF16)

    def seg(i):
        return _dot(xb, win_ref[:, _SEG_O[i]:_SEG_O[i] + _SEG_W[i]])

    cosr, sinr = cosr_ref[...], sinr_ref[...]
    cm, sa, sb = cm_ref[...], sa_ref[...], sb_ref[...]

    def ret_rotary(z, out_ref, scale):
        for h in range(RET_HEADS):
            zh = z[:, h * RET_DK:(h + 1) * RET_DK]
            r = zh * cosr + pltpu.roll(zh, RET_DK // 2, 1) * sinr
            if scale is not None:
                r = r * scale
            out_ref[:, h * RET_DK:(h + 1) * RET_DK] = r.astype(out_ref.dtype)

    def mla_rotary(zh):
        half = ROPE_DIM // 2
        return zh * cm + pltpu.roll(zh, HEAD_PAD - half, 1) * sa + pltpu.roll(zh, half, 1) * sb

    def head_norm(zh, g):
        ss = jnp.sum(zh * zh, axis=-1, keepdims=True)
        return zh * lax.rsqrt(ss * (1.0 / QK_DIM) + EPS) * g

    qa = seg(4)
    qn = qa * lax.rsqrt(jnp.mean(qa * qa, axis=-1, keepdims=True) + EPS) * gqa_ref[...]
    ckv = seg(5)
    c = ckv * lax.rsqrt(jnp.mean(ckv * ckv, axis=-1, keepdims=True) + EPS) * gkva_ref[...]
    c_ref[...] = c
    cb = c.astype(BF16)
    krp = mla_rotary(seg(6))
    kr_ref[...] = krp[:, NOPE_DIM:QK_DIM]

    ret_rotary(seg(0), rq_ref, None)
    ret_rotary(seg(1), rk_ref, RET_DK ** -0.5)
    rv_ref[...] = seg(2).astype(rv_ref.dtype)
    rg_ref[...] = seg(3).astype(rg_ref.dtype)

    qp = _dot(qn.astype(BF16), wqb_ref[...])
    gr_ref[...] = seg(7).astype(gr_ref.dtype)
    gq = gq_ref[...]
    for h in range(MLA_HEADS):
        qh = mla_rotary(qp[:, h * HEAD_PAD:(h + 1) * HEAD_PAD])
        q_ref[:, h * HEAD_PAD:(h + 1) * HEAD_PAD] = head_norm(qh, gq).astype(q_ref.dtype)

    kp = _dot(cb, wukp_ref[...])
    gk = gk_ref[...]
    for h in range(MLA_HEADS):
        kh = kp[:, h * HEAD_PAD:(h + 1) * HEAD_PAD] + krp
        k_ref[:, h * HEAD_PAD:(h + 1) * HEAD_PAD] = head_norm(kh, gk).astype(k_ref.dtype)
    vt_w = vt_ref.shape[-1]
    for t in range(vt_ref.shape[0]):
        vt_ref[t] = _dot_nt(wuvt_ref[...], cb[t * vt_w:(t + 1) * vt_w]).astype(vt_ref.dtype)

    def branch_gates():
        gm_ref[...] = seg(8).astype(gm_ref.dtype)

    if tiles_per_seq is None:
        branch_gates()
        return

    for ck in range(x_ref.shape[0] // RET_CHUNK):
        rows = slice(ck * RET_CHUNK, (ck + 1) * RET_CHUNK)
        for h in range(RET_HEADS):
            qh = rq_ref[rows, h * RET_DK:(h + 1) * RET_DK]
            kh = rk_ref[rows, h * RET_DK:(h + 1) * RET_DK]
            vh = rv_ref[rows, h * RET_DV:(h + 1) * RET_DV]
            scores = _dot_nt(qh, kh) * dec_ref[h]
            o = _dot(scores.astype(BF16), vh)
            s_prev = s_scr[h]
            o = o + _dot(qh, s_prev.astype(BF16)) * qdec_ref[h]
            kd = (kh.astype(F32) * kdec_ref[h]).astype(BF16)
            s_scr[h] = sdec_ref[h] * s_prev + lax.dot_general(kd, vh, _TN, preferred_element_type=F32)
            on = o * lax.rsqrt(jnp.mean(o * o, axis=-1, keepdims=True) + EPS)
            g = rg_ref[rows, h * RET_DV:(h + 1) * RET_DV].astype(F32)
            ret_ref[rows, h * RET_DV:(h + 1) * RET_DV] = (g * jax.nn.sigmoid(g) * on).astype(ret_ref.dtype)
    branch_gates()

    @pl.when(tile == tiles_per_seq - 1)
    def _():
        st_ref[0] = s_scr[...]


def _proj(x, tabs, w, tm, vt_w, mid_dtype, retention=None):
    rows = x.shape[0]
    period = tabs[0].shape[0] // tm
    row = lambda i: (i, 0)
    tab_spec = pl.BlockSpec((tm, LANES), lambda i: (i % period, 0))
    st_shape = (RET_HEADS, RET_DK, RET_DV)
    ret_widths = (RET_QK_W, RET_QK_W, RET_V_W, RET_V_W)
    widths = ((RET_V_W,) if retention else ret_widths) + (D_MODEL, D_MODEL, MLA_PAD_W, MLA_PAD_W)
    out_shape = [jax.ShapeDtypeStruct((rows, n), mid_dtype) for n in widths]
    out_shape += [jax.ShapeDtypeStruct((rows // vt_w, MLA_V_W, vt_w), mid_dtype),
                  jax.ShapeDtypeStruct((rows, KV_LORA), F32), jax.ShapeDtypeStruct((rows, ROPE_DIM), F32)]
    out_specs = [pl.BlockSpec((tm, n), row) for n in widths]
    out_specs += [pl.BlockSpec((tm // vt_w, MLA_V_W, vt_w), lambda i: (i, 0, 0)),
                  pl.BlockSpec((tm, KV_LORA), row), pl.BlockSpec((tm, ROPE_DIM), row)]
    consts = (w['norm_mix'], w['w_in'], w['g_qa'], w['w_qb'], w['g_kva'], w['w_ukp'], w['w_uv_t'],
              w['gq'], w['gk'])
    scratch, tiles_per_seq = [], None
    if retention:
        seq_rows, dtabs, s0 = retention
        tiles_per_seq = seq_rows // tm
        consts += tuple(dtabs) + (s0,)
        out_shape.append(jax.ShapeDtypeStruct((rows // seq_rows,) + st_shape, F32))
        out_specs.append(pl.BlockSpec((1,) + st_shape, lambda i: (i // tiles_per_seq, 0, 0, 0)))
        scratch = [pltpu.VMEM((tm, n), mid_dtype) for n in ret_widths] + [pltpu.VMEM(st_shape, F32)]
    return pl.pallas_call(
        functools.partial(_proj_kernel, tiles_per_seq=tiles_per_seq),
        grid=(rows // tm,),
        in_specs=[pl.BlockSpec((tm, D_MODEL), row)] + [tab_spec] * 5 + [_const_spec(a.shape) for a in consts],
        out_specs=out_specs,
        out_shape=out_shape,
        scratch_shapes=scratch,
        compiler_params=_params(("arbitrary" if retention else "parallel",)),
        name="proj",
    )(x, *tabs, *consts)


ATT_TQ = 1024
ATT_TK = 256


def _attention_kernel(q_ref, k_ref, vt_ref, km_ref, vmt_ref, o_ref, qt_scr, s_scr, m_scr, l_scr, acc_scr):
    qi = pl.program_id(1)
    meta_rows = km_ref.shape[0]
    hs = lambda h: slice(h * HEAD_PAD, (h + 1) * HEAD_PAD)
    vs = lambda h: slice(h * V_DIM, (h + 1) * V_DIM)

    qt_scr[...] = q_ref[...].T
    m_scr[...] = jnp.full(m_scr.shape, -jnp.inf, F32)
    l_scr[...] = jnp.zeros(l_scr.shape, F32)
    acc_scr[...] = jnp.zeros(acc_scr.shape, F32)

    def update(n, keys, values_t, causal=False, c0=0):
        cols = slice(c0, ATT_TQ)
        for h in range(MLA_HEADS):
            s_scr[h, 0:n, cols] = _dot(keys(h), qt_scr[hs(h), cols])
        if causal:
            mask = (lax.broadcasted_iota(jnp.int32, (n, ATT_TQ - c0), 0)
                    <= lax.broadcasted_iota(jnp.int32, (n, ATT_TQ - c0), 1))
        for h in range(MLA_HEADS):
            s = s_scr[h, 0:n, cols]
            if causal:
                s = jnp.where(mask, s, -jnp.inf)
            m_old = m_scr[h, :, cols]
            m_new = jnp.maximum(m_old, jnp.max(s, axis=0, keepdims=True))
            alpha = jnp.exp2(m_old - m_new)
            p = jnp.exp2(s - m_new)
            m_scr[h, :, cols] = m_new
            l_scr[h, :, cols] = alpha * l_scr[h, :, cols] + jnp.sum(p, axis=0, keepdims=True)
            acc_scr[vs(h), cols] = alpha * acc_scr[vs(h), cols] + _dot(values_t(h), p.astype(BF16))

    update(meta_rows, lambda h: km_ref[:, hs(h)], lambda h: vmt_ref[vs(h), :])

    def key_tile(j, **kw):
        start = pl.multiple_of(j * ATT_TK, ATT_TK)
        update(ATT_TK, lambda h: k_ref[pl.ds(start, ATT_TK), hs(h)], lambda h: vt_ref[j, vs(h), :], **kw)

    def body(j, carry):
        key_tile(j)
        return carry

    first_diag = qi * (ATT_TQ // ATT_TK)
    lax.fori_loop(0, first_diag, body, 0)
    for d in range(ATT_TQ // ATT_TK):
        key_tile(first_diag + d, causal=True, c0=d * ATT_TK)

    for h in range(MLA_HEADS):
        acc_scr[vs(h), :] = acc_scr[vs(h), :] / l_scr[h]
    o_ref[...] = acc_scr[...].T.astype(o_ref.dtype)


def _attention(q, k, vt, km, vmt):
    nb, s = q.shape[:2]
    return pl.pallas_call(
        _attention_kernel,
        grid=(nb, s // ATT_TQ),
        in_specs=[pl.BlockSpec((None, ATT_TQ, MLA_PAD_W), lambda b, i: (b, i, 0)),
                  pl.BlockSpec((None, s, MLA_PAD_W), lambda b, i: (b, 0, 0)),
                  pl.BlockSpec((None, s // ATT_TK, MLA_V_W, ATT_TK), lambda b, i: (b, 0, 0, 0)),
                  _const_spec(km.shape), _const_spec(vmt.shape)],
        out_specs=pl.BlockSpec((None, ATT_TQ, MLA_V_W), lambda b, i: (b, i, 0)),
        out_shape=jax.ShapeDtypeStruct((nb, s, MLA_V_W), BF16),
        scratch_shapes=[pltpu.VMEM((MLA_PAD_W, ATT_TQ), BF16),
                        pltpu.VMEM((MLA_HEADS, ATT_TK, ATT_TQ), F32),
                        pltpu.VMEM((MLA_HEADS, 1, ATT_TQ), F32), pltpu.VMEM((MLA_HEADS, 1, ATT_TQ), F32),
                        pltpu.VMEM((MLA_V_W, ATT_TQ), F32)],
        compiler_params=_params(("parallel", "arbitrary")),
        name="attention",
    )(q, k, vt, km, vmt)


FF_BLOCK = 1024


def _finish_kernel(x_ref, ret_ref, mla_ref, gr_ref, gm_ref, wro_ref, wmo_ref, wout_ref, nffn_ref,
                   wup_ref, wdn_ref, y_ref):
    r = _dot(ret_ref[...].astype(BF16), wro_ref[...])
    m = _dot(mla_ref[...].astype(BF16), wmo_ref[...])
    mixed = jax.nn.sigmoid(gr_ref[...].astype(F32)) * r + jax.nn.sigmoid(gm_ref[...].astype(F32)) * m
    h = x_ref[...] + _dot(mixed.astype(BF16), wout_ref[...])
    hn = h * lax.rsqrt(jnp.mean(h * h, axis=-1, keepdims=True) + EPS) * nffn_ref[...]
    hb = hn.astype(BF16)
    y = h
    for f in range(D_FF // FF_BLOCK):
        u = jnp.maximum(_dot(hb, wup_ref[:, f * FF_BLOCK:(f + 1) * FF_BLOCK]), 0.0)
        y = y + _dot((u * u).astype(BF16), wdn_ref[f * FF_BLOCK:(f + 1) * FF_BLOCK, :])
    y_ref[...] = y


def _finish(x, ret_mid, o_mla, gr, gm, w, tm):
    rows = x.shape[0]
    row = lambda i: (i, 0)
    consts = (w['w_ret_o'], w['w_mla_o'], w['w_out'], w['norm_ffn'], w['w_up'], w['w_down'])
    return pl.pallas_call(
        _finish_kernel,
        grid=(rows // tm,),
        in_specs=[pl.BlockSpec((tm, D_MODEL), row), pl.BlockSpec((tm, RET_V_W), row),
                  pl.BlockSpec((tm, MLA_V_W), row), pl.BlockSpec((tm, D_MODEL), row),
                  pl.BlockSpec((tm, D_MODEL), row)] + [_const_spec(a.shape) for a in consts],
        out_specs=pl.BlockSpec((tm, D_MODEL), row),
        out_shape=jax.ShapeDtypeStruct((rows, D_MODEL), F32),
        compiler_params=_params(("parallel",)),
        name="finish",
    )(x, ret_mid, o_mla, gr, gm, *consts)


RET_DEC_GROUP = 8


def _column(row_vec, eye):
    return jnp.sum(jnp.where(eye, row_vec, 0.0), axis=-1, keepdims=True)


def _ret_decode_kernel(q_ref, k_ref, v_ref, rg_ref, gam_ref, s_ref, o_ref, sn_ref):
    eye = (lax.broadcasted_iota(jnp.int32, (RET_DK, RET_DK), 0)
           == lax.broadcasted_iota(jnp.int32, (RET_DK, RET_DK), 1))
    for b in range(RET_DEC_GROUP):
        for h in range(RET_HEADS):
            k_col = _column(k_ref[b:b + 1, h * RET_DK:(h + 1) * RET_DK].astype(F32), eye)
            v_row = v_ref[b:b + 1, h * RET_DV:(h + 1) * RET_DV].astype(F32)
            s_new = gam_ref[h] * s_ref[b, h] + k_col * v_row
            sn_ref[b, h] = s_new
            q8 = q_ref[:, h * RET_DK:(h + 1) * RET_DK].astype(BF16)
            o = _dot(q8, s_new.astype(BF16))[b:b + 1]
            on = o * lax.rsqrt(jnp.mean(o * o, axis=-1, keepdims=True) + EPS)
            g = rg_ref[b:b + 1, h * RET_DV:(h + 1) * RET_DV].astype(F32)
            o_ref[b:b + 1, h * RET_DV:(h + 1) * RET_DV] = (g * jax.nn.sigmoid(g) * on).astype(o_ref.dtype)


def _ret_decode(rq, rk, rv, rg, gam, state):
    nb = rq.shape[0]
    g = RET_DEC_GROUP
    row = lambda i: (i, 0)
    st_spec = pl.BlockSpec((g, RET_HEADS, RET_DK, RET_DV), lambda i: (i, 0, 0, 0))
    return pl.pallas_call(
        _ret_decode_kernel,
        grid=(nb // g,),
        in_specs=[pl.BlockSpec((g, RET_QK_W), row), pl.BlockSpec((g, RET_QK_W), row),
                  pl.BlockSpec((g, RET_V_W), row), pl.BlockSpec((g, RET_V_W), row),
                  _const_spec(gam.shape), st_spec],
        out_specs=[pl.BlockSpec((g, RET_V_W), row), st_spec],
        out_shape=[jax.ShapeDtypeStruct((nb, RET_V_W), F32), jax.ShapeDtypeStruct(state.shape, F32)],
        compiler_params=_params(("parallel",)),
        name="ret_decode",
    )(rq, rk, rv, rg, gam, state)


def _absorb_kernel(q_ref, k_ref, gk_ref, wukp_ref, qt_ref, qg_ref, s0_ref):
    q = q_ref[...]
    lane = lax.broadcasted_iota(jnp.int32, (q.shape[0], LANES), 1)
    s0 = jnp.zeros((q.shape[0], LANES), F32)
    gk = gk_ref[...]
    for h in range(MLA_HEADS):
        sl = slice(h * HEAD_PAD, (h + 1) * HEAD_PAD)
        qh = q[:, sl]
        s0 = s0 + jnp.where(lane == h, jnp.sum(qh * k_ref[:, sl], axis=-1, keepdims=True), 0.0)
        qg = qh * gk
        qg_ref[:, sl] = qg
        qt_ref[h] = lax.dot_general(qg, wukp_ref[:, sl], _NT, precision=lax.Precision.HIGHEST,
                                    preferred_element_type=F32)
    s0_ref[...] = s0


def _absorb(q, k, gk, wukp):
    nb = q.shape[0]
    return pl.pallas_call(
        _absorb_kernel,
        out_shape=[jax.ShapeDtypeStruct((MLA_HEADS, nb, KV_LORA), F32),
                   jax.ShapeDtypeStruct((nb, MLA_PAD_W), F32),
                   jax.ShapeDtypeStruct((nb, LANES), F32)],
        compiler_params=pltpu.CompilerParams(vmem_limit_bytes=VMEM_LIMIT),
        name="absorb",
    )(q, k, gk, wukp)


DEC_PAGES = 32
DEC_T = DEC_PAGES * PAGE_SIZE
DEC_SUB = 512
UK_ROWS = MLA_HEADS * NOPE_DIM
QT_ROWS = 16
DEC_AHEAD = 2
DEC_SLOTS = DEC_AHEAD + 1


def _mla_decode_kernel(pt_ref, lhsw_ref, qt_ref, qrope_ref, s0_ref, cnew_ref, wuv_ref, ckv_hbm, krt_hbm,
                       o_ref, lhs_scr, cbuf, kbuf, cb_scr, big_scr, s_st, m_st, l_st, acc_st, sem):
    b = pl.program_id(0)
    nb = pl.num_programs(0) - 1
    n_groups = pt_ref.shape[1] // DEC_PAGES

    def page_copies(seq, g, slot):
        copies = []
        for i in range(DEC_PAGES):
            page = pt_ref[seq, g * DEC_PAGES + i]
            tok = pl.ds(i * PAGE_SIZE, PAGE_SIZE)
            copies.append(pltpu.make_async_copy(ckv_hbm.at[0, page], cbuf.at[slot, tok], sem.at[0, slot]))
            copies.append(pltpu.make_async_copy(krt_hbm.at[0, page], kbuf.at[slot, :, tok], sem.at[1, slot]))
        return copies

    def start_group(seq, g, slot):
        for cp in page_copies(seq, g, slot):
            cp.start()

    def wait_group(slot):
        for cp in page_copies(0, 0, slot):
            cp.wait()

    def scores(slot, cb_slot):
        cb = cbuf[slot].astype(BF16)
        cb_scr[cb_slot] = cb
        big_scr[...] = _dot_nt(lhs_scr[...], cb)
        rope = _dot(qrope_ref[0], kbuf[slot].astype(BF16))
        pieces = []
        for t in range(DEC_T // DEC_SUB):
            tok = pl.ds(t * DEC_SUB, DEC_SUB)
            sq = [big_scr[d * MLA_HEADS:(d + 1) * MLA_HEADS, tok] for d in range(NOPE_DIM)]
            sq = [x * x for x in sq]
            while len(sq) > 1:
                sq = [sq[i] + sq[i + 1] for i in range(0, len(sq), 2)]
            krt = kbuf[slot, :, tok]
            kk = krt * krt
            kk = (kk[0:8] + kk[8:16]) + (kk[16:24] + kk[24:32])
            ss = sq[0] + jnp.sum(kk, axis=0, keepdims=True)
            num = big_scr[UK_ROWS:UK_ROWS + MLA_HEADS, tok] + rope[0:MLA_HEADS, t * DEC_SUB:(t + 1) * DEC_SUB]
            pieces.append(num * lax.rsqrt(ss * (1.0 / QK_DIM) + EPS))
        return jnp.concatenate(pieces, axis=-1)

    def softmax_update(s, slot, m, l, acc):
        m_new = jnp.maximum(m, jnp.max(s, axis=-1, keepdims=True))
        alpha = jnp.exp2(m - m_new)
        p = jnp.exp2(s - m_new)
        l = alpha * l + jnp.sum(p, axis=-1, keepdims=True)
        acc = alpha * acc + _dot(p.astype(BF16), cb_scr[slot])
        return m_new, l, acc

    last = nb * n_groups - 1

    def fetch(n):
        src = jnp.minimum(n, last)
        start_group(lax.div(src, n_groups), lax.rem(src, n_groups), lax.rem(n, DEC_SLOTS))

    def arrive(n):
        slot = lax.rem(n, DEC_SLOTS)
        wait_group(slot)
        return slot

    first = b * n_groups
    last_cb = (n_groups - 1) % 2

    def finish_previous():
        _, l, acc = softmax_update(s_st[...], last_cb, m_st[...], l_st[...], acc_st[...])
        lat = acc / l
        full = _dot(lat.astype(BF16), wuv_ref[...])
        own = (lax.broadcasted_iota(jnp.int32, full.shape, 1) // V_DIM
               == lax.broadcasted_iota(jnp.int32, full.shape, 0))
        o_ref[0] = jnp.sum(jnp.where(own, full, 0.0), axis=0, keepdims=True)

    @pl.when(b == 0)
    def _():
        for n in range(DEC_AHEAD):
            start_group(0, n, n)
        s_st[...] = jnp.zeros(s_st.shape, F32)
        m_st[...] = jnp.zeros(m_st.shape, F32)
        l_st[...] = jnp.ones(l_st.shape, F32)
        acc_st[...] = jnp.zeros(acc_st.shape, F32)
        cb_scr[last_cb] = jnp.zeros(cb_scr.shape[1:], BF16)

    @pl.when(b < nb)
    def _():
        lhs_scr[0:UK_ROWS, :] = lhsw_ref[...]
        lhs_scr[UK_ROWS:UK_ROWS + QT_ROWS, :] = qt_ref[0]

        s_first = scores(arrive(first), 0)
        fetch(first + DEC_AHEAD)
        finish_previous()

        def body(g, carry):
            s_prev, m, l, acc = carry
            s_cur = scores(arrive(first + g), g % 2)
            m, l, acc = softmax_update(s_prev, 1 - g % 2, m, l, acc)
            fetch(first + g + DEC_AHEAD)
            return s_cur, m, l, acc

        init = (s_first, s0_ref[0], jnp.ones((MLA_HEADS, 1), F32),
                jnp.broadcast_to(cnew_ref[0], (MLA_HEADS, KV_LORA)))
        s_st[...], m_st[...], l_st[...], acc_st[...] = lax.fori_loop(1, n_groups, body, init)

    @pl.when(b == nb)
    def _():
        finish_previous()
        for n in range(1, DEC_AHEAD + 1):
            wait_group(lax.rem(last + n, DEC_SLOTS))


def _mla_decode(page_table, lhsw, qt, qrope, s0, c_new, wuv, cache_ckv, cache_krt):
    nb, n_pages = page_table.shape
    n_groups = n_pages // DEC_PAGES
    assert n_pages % DEC_PAGES == 0 and n_groups >= DEC_AHEAD and n_groups % 2 == 0
    per_seq = lambda shape: pl.BlockSpec((1,) + shape, lambda i, pt: (jnp.minimum(i, nb - 1), 0, 0))
    const = lambda a: pl.BlockSpec(a.shape, lambda i, pt: (0,) * a.ndim, pipeline_mode=pl.Buffered(1))
    grid_spec = pltpu.PrefetchScalarGridSpec(
        num_scalar_prefetch=1,
        grid=(nb + 1,),
        in_specs=[const(lhsw), per_seq((QT_ROWS, KV_LORA)), per_seq((QT_ROWS, ROPE_DIM)),
                  per_seq((MLA_HEADS, 1)), per_seq((1, KV_LORA)), const(wuv),
                  pl.BlockSpec(memory_space=pl.ANY), pl.BlockSpec(memory_space=pl.ANY)],
        out_specs=pl.BlockSpec((1, 1, MLA_V_W), lambda i, pt: (jnp.maximum(i - 1, 0), 0, 0)),
        scratch_shapes=[pltpu.VMEM((UK_ROWS + QT_ROWS, KV_LORA), BF16),
                        pltpu.VMEM((DEC_SLOTS, DEC_T, KV_LORA), F32),
                        pltpu.VMEM((DEC_SLOTS, ROPE_DIM, DEC_T), F32),
                        pltpu.VMEM((2, DEC_T, KV_LORA), BF16),
                        pltpu.VMEM((UK_ROWS + QT_ROWS, DEC_T), F32),
                        pltpu.VMEM((MLA_HEADS, DEC_T), F32), pltpu.VMEM((MLA_HEADS, 1), F32),
                        pltpu.VMEM((MLA_HEADS, 1), F32), pltpu.VMEM((MLA_HEADS, KV_LORA), F32),
                        pltpu.SemaphoreType.DMA((2, DEC_SLOTS))],
    )
    return pl.pallas_call(
        _mla_decode_kernel,
        grid_spec=grid_spec,
        out_shape=jax.ShapeDtypeStruct((nb, 1, MLA_V_W), F32),
        compiler_params=_params(("arbitrary",)),
        name="mla_decode",
    )(page_table, lhsw, qt, qrope, s0, c_new, wuv, cache_ckv, cache_krt)


def _rotary_tables(pos):
    pos = pos.astype(F32)[:, None]

    def cos_sin(half):
        inv = ROPE_BASE ** (-jnp.arange(half, dtype=F32) / half)
        ang = pos * inv[None, :]
        return jnp.cos(ang), jnp.sin(ang)

    cr, sr = cos_sin(RET_DK // 2)
    cosr = jnp.concatenate([cr, cr], axis=-1)
    sinr = jnp.concatenate([-sr, sr], axis=-1)
    c, s = cos_sin(ROPE_DIM // 2)
    n = pos.shape[0]
    half = ROPE_DIM // 2
    z = lambda w: jnp.zeros((n, w), F32)
    cm = jnp.concatenate([jnp.ones((n, NOPE_DIM), F32), c, c, z(HEAD_PAD - QK_DIM)], axis=-1)
    sa = jnp.concatenate([z(NOPE_DIM), -s, z(half), z(HEAD_PAD - QK_DIM)], axis=-1)
    sb = jnp.concatenate([z(NOPE_DIM), z(half), s, z(HEAD_PAD - QK_DIM)], axis=-1)
    return cosr, sinr, cm, sa, sb


def _decay_tables(c):
    lg = jnp.log1p(-jnp.exp2(-5.0 - jnp.arange(RET_HEADS, dtype=F32)))
    idx = jnp.arange(c, dtype=F32)
    diff = idx[:, None] - idx[None, :]
    dec = jnp.where(diff[None] >= 0, jnp.exp(jnp.maximum(diff, 0.0)[None] * lg[:, None, None]), 0.0)
    qdec = jnp.exp((idx + 1.0)[None, :] * lg[:, None])[:, :, None]
    kdec = jnp.exp((c - 1.0 - idx)[None, :] * lg[:, None])[:, :, None]
    sdec = jnp.exp(c * lg)[:, None, None]
    return dec, qdec, kdec, sdec


def _pad_heads(a):
    a = jnp.pad(a, [(0, 0)] * (a.ndim - 1) + [(0, HEAD_PAD - a.shape[-1])])
    return a.reshape(a.shape[:-2] + (a.shape[-2] * HEAD_PAD,))


def _layer_weights(norm_mix, w_in, g_qa, w_qb, g_kva, w_uk, w_uv, g_qn, g_kn, w_ret_o, w_mla_o, w_out,
                   norm_ffn, w_up, w_down):
    idx = [int(i) for i in np.cumsum(SPLITS)[:-1]]
    parts = jnp.split(w_in, idx, axis=-1)
    parts[6] = jnp.pad(parts[6], ((0, 0), (NOPE_DIM, HEAD_PAD - QK_DIM)))
    pad1 = lambda g: jnp.pad(g, (0, HEAD_PAD - QK_DIM))[None, :]
    wukp = _pad_heads(w_uk)
    return {
        'norm_mix': norm_mix[None, :],
        'w_in': jnp.concatenate([parts[i] for i in _SEG_ORDER], axis=-1).astype(BF16),
        'g_qa': g_qa[None, :],
        'w_qb': _pad_heads(w_qb.reshape(Q_LORA, MLA_HEADS, QK_DIM)).astype(BF16),
        'g_kva': g_kva[None, :],
        'w_ukp': wukp.astype(BF16),
        'w_ukp_f32': wukp,
        'w_uk_t': w_uk.transpose(2, 1, 0).reshape(UK_ROWS, KV_LORA).astype(BF16),
        'w_uv': w_uv.reshape(KV_LORA, MLA_V_W).astype(BF16),
        'w_uv_t': w_uv.reshape(KV_LORA, MLA_V_W).T.astype(BF16),
        'gq': pad1(g_qn) * (QK_DIM ** -0.5 * LOG2_E),
        'gk': pad1(g_kn),
        'w_ret_o': w_ret_o.astype(BF16),
        'w_mla_o': w_mla_o.astype(BF16),
        'w_out': w_out.astype(BF16),
        'norm_ffn': norm_ffn[None, :],
        'w_up': w_up.astype(BF16),
        'w_down': w_down.astype(BF16),
    }


PROJ_TM = 256
FINISH_TM = 512


def kernel(x_prompt, x_sample, cache_ckv, cache_krope, state_ret, page_table, meta_tokens, norm_mix, w_in,
           g_qa, w_qb, g_kva, w_uk, w_uv, g_qn, g_kn, w_ret_o, w_mla_o, w_out, norm_ffn, w_up, w_down):
    nb, seq, _ = x_prompt.shape
    ns = x_sample.shape[0]
    assert x_sample.shape[1] == 1 and norm_mix.shape[0] == 1
    past = page_table.shape[1] * PAGE_SIZE
    w = _layer_weights(norm_mix[0], w_in[0], g_qa[0], w_qb[0], g_kva[0], w_uk[0], w_uv[0], g_qn[0], g_kn[0],
                       w_ret_o[0], w_mla_o[0], w_out[0], norm_ffn[0], w_up[0], w_down[0])
    dtabs = _decay_tables(RET_CHUNK)

    mpad = RET_CHUNK - N_META
    xm = jnp.pad(meta_tokens, ((mpad, 0), (0, 0)))
    pos_m = jnp.maximum(jnp.arange(RET_CHUNK) - mpad, 0)
    zero_state = jnp.zeros((1, RET_HEADS, RET_DK, RET_DV), F32)
    _, _, _, _, km, vmt, c_m, kr_m, s0 = _proj(xm, _rotary_tables(pos_m), w, RET_CHUNK, RET_CHUNK, BF16,
                                               retention=(RET_CHUNK, dtabs, zero_state))

    xp = x_prompt.reshape(nb * seq, D_MODEL)
    ret_mid, gr, gm, q, k, vt, c_p, kr_p, st_p = _proj(
        xp, _rotary_tables(N_META + jnp.arange(seq)), w, PROJ_TM, ATT_TK, BF16, retention=(seq, dtabs, s0))
    b3 = lambda a: a.reshape(nb, seq, a.shape[-1])
    o_mla = _attention(b3(q), b3(k), vt.reshape(nb, seq // ATT_TK, MLA_V_W, ATT_TK), km[mpad:],
                       vmt[0][:, mpad:])
    y_prompt = _finish(xp, ret_mid, o_mla.reshape(nb * seq, MLA_V_W), gr, gm, w,
                       FINISH_TM).reshape(nb, seq, D_MODEL)
    bcast = lambda a: jnp.broadcast_to(a[None, mpad:], (nb, N_META, a.shape[-1]))
    ckv_prompt = jnp.concatenate([bcast(c_m), b3(c_p)], axis=1)[None]
    krope_prompt = jnp.concatenate([bcast(kr_m), b3(kr_p)], axis=1)[None]

    xs = x_sample.reshape(ns, D_MODEL)
    pos_s = jnp.full((ns,), past, jnp.int32)
    rq, rk, rv, rg, gr, gm, q, k, _, c_s, kr_s = _proj(xs, _rotary_tables(pos_s), w, ns, ns, F32)
    gam = jnp.exp(jnp.log1p(-jnp.exp2(-5.0 - jnp.arange(RET_HEADS, dtype=F32))))[:, None, None]
    ret_mid, st_s = _ret_decode(rq, rk, rv, rg, gam, state_ret[0])
    qt, qg, s0_new = _absorb(q, k, w['gk'], w['w_ukp_f32'])
    qt = jnp.pad(qt.transpose(1, 0, 2), ((0, 0), (0, QT_ROWS - MLA_HEADS), (0, 0))).astype(BF16)
    q_rope = qg.reshape(ns, MLA_HEADS, HEAD_PAD)[:, :, NOPE_DIM:QK_DIM]
    q_rope = jnp.pad(q_rope, ((0, 0), (0, QT_ROWS - MLA_HEADS), (0, 0))).astype(BF16)
    o_mla = _mla_decode(page_table, w['w_uk_t'], qt, q_rope, s0_new[:, :MLA_HEADS, None], c_s[:, None, :],
                        w['w_uv'], cache_ckv, jnp.swapaxes(cache_krope, 2, 3))
    y_sample = _finish(xs, ret_mid, o_mla.reshape(ns, MLA_V_W), gr, gm, w, ns).reshape(ns, 1, D_MODEL)

    return (y_prompt, y_sample, ckv_prompt, krope_prompt, st_p[None],
            c_s.reshape(1, ns, 1, KV_LORA), kr_s.reshape(1, ns, 1, ROPE_DIM), st_s[None])
```
